```python
import jax, jax.numpy as jnp
from jax import lax
import numpy as np

D_MODEL = 1024
BATCH = 8
SEQ = 2048
DEPTH = 2

CHUNK = 64
Q_BLOCK = 2 * CHUNK
D_MIX = D_MODEL
D_CONV = D_MIX // 4
D_POOL = D_MIX // 4
D_ATTN = D_MIX // 2
N_HEADS = 8
HEAD_DIM = D_ATTN // N_HEADS
CONV_K = 31
POOL_WINDOWS = (2, 4, 8, 16)
N_POOL_GROUPS = len(POOL_WINDOWS)
POOL_GROUP = D_POOL // N_POOL_GROUPS
D_FF = ((8 * D_MODEL // 3 + 127) // 128) * 128
FFN_CONV_K = 3
D_IN = 2 * D_CONV + D_POOL + 3 * D_ATTN
EPS = 1e-6

kernel_name = "hybrid_conv_pool_stickbreak_block"


def rmsnorm(x, g):
    xf = x.astype(jnp.float32)
    y = xf * lax.rsqrt(jnp.mean(xf * xf, axis=-1, keepdims=True) + EPS)
    return (y * g.astype(jnp.float32)).astype(x.dtype)


def layernorm(x, g, b):
    xf = x.astype(jnp.float32)
    mu = jnp.mean(xf, axis=-1, keepdims=True)
    var = jnp.mean(jnp.square(xf - mu), axis=-1, keepdims=True)
    y = (xf - mu) * lax.rsqrt(var + EPS)
    return (y * g.astype(jnp.float32) + b.astype(jnp.float32)).astype(x.dtype)


def causal_dwconv(x, w, b):
    k, c = w.shape
    y = lax.conv_general_dilated(
        x, w[:, None, :].astype(x.dtype), window_strides=(1,), padding=[(k - 1, 0)],
        dimension_numbers=("NWC", "WIO", "NWC"), feature_group_count=c)
    return y + b.astype(x.dtype)


def conformer_conv(a, gate, dw_w, dw_b, ln_g, ln_b, pw_w, pw_b):
    u = a * jax.nn.sigmoid(gate)
    u = causal_dwconv(u, dw_w, dw_b)
    u = jax.nn.silu(layernorm(u, ln_g, ln_b))
    return u @ pw_w + pw_b


def multiscale_pool(u, w, scale):
    b, s, _ = u.shape
    uf = u.astype(jnp.float32)
    cs = jnp.cumsum(uf, axis=1)
    count = jnp.arange(1, s + 1, dtype=jnp.float32)[None, :, None]
    groups = []
    for g, win in enumerate(POOL_WINDOWS):
        sl = slice(g * POOL_GROUP, (g + 1) * POOL_GROUP)
        c = cs[..., sl]
        lag = jnp.pad(c, ((0, 0), (win, 0), (0, 0)))[:, :s]
        groups.append((c - lag) / jnp.minimum(count, float(win)) - uf[..., sl])
    p = jnp.stack(groups, axis=2).astype(u.dtype)
    y = jnp.einsum("bsgc,gcd->bsgd", p, w).reshape(b, s, D_POOL)
    return y * scale


def stick_breaking_attention(q, k, v):
    s_len = q.shape[2]
    scale = HEAD_DIM ** -0.5
    outs = []
    for i0 in range(0, s_len, Q_BLOCK):
        kv = i0 + Q_BLOCK
        z = jnp.einsum("bhqd,bhkd->bhqk", q[:, :, i0:kv], k[:, :, :kv]).astype(jnp.float32) * scale
        t_pos = i0 + jnp.arange(Q_BLOCK)[:, None]
        s_pos = jnp.arange(kv)[None, :]
        mask = s_pos < t_pos
        log_keep = jnp.where(mask, jax.nn.log_sigmoid(-z), 0.0)
        log_stick = lax.cumsum(log_keep, axis=3, reverse=True) - log_keep
        a = jnp.where(mask, jnp.exp(jax.nn.log_sigmoid(z) + log_stick), 0.0)
        outs.append(jnp.einsum("bhqk,bhkd->bhqd", a.astype(v.dtype), v[:, :, :kv]))
    return jnp.concatenate(outs, axis=2)


def setup_inputs(seed: int = 0) -> dict:
    key = jax.random.key(seed)
    ks = jax.random.split(key, 20)
    f32 = jnp.float32
    nrm = lambda k, shape, s: jax.random.normal(k, shape, f32) * s
    return {
        "x": jax.random.normal(ks[0], (BATCH, SEQ, D_MODEL), f32),
        "norm1_g": 1.0 + nrm(ks[1], (DEPTH, D_MODEL), 0.05),
        "w_in": nrm(ks[2], (DEPTH, D_MODEL, D_IN), D_MODEL ** -0.5),
        "conv_dw_w": nrm(ks[3], (DEPTH, CONV_K, D_CONV), CONV_K ** -0.5),
        "conv_dw_b": nrm(ks[4], (DEPTH, D_CONV), 0.02),
        "conv_ln_g": 1.0 + nrm(ks[5], (DEPTH, D_CONV), 0.05),
        "conv_ln_b": nrm(ks[6], (DEPTH, D_CONV), 0.02),
        "conv_pw_w": nrm(ks[7], (DEPTH, D_CONV, D_CONV), D_CONV ** -0.5),
        "conv_pw_b": nrm(ks[8], (DEPTH, D_CONV), 0.02),
        "pool_w": nrm(ks[9], (DEPTH, N_POOL_GROUPS, POOL_GROUP, POOL_GROUP), POOL_GROUP ** -0.5),
        "pool_scale": 1.0 + nrm(ks[10], (DEPTH, D_POOL), 0.1),
        "w_out": nrm(ks[11], (DEPTH, D_MIX, D_MODEL), D_MIX ** -0.5),
        "norm2_g": 1.0 + nrm(ks[12], (DEPTH, D_MODEL), 0.05),
        "ffn_up": nrm(ks[13], (DEPTH, D_MODEL, 2 * D_FF), D_MODEL ** -0.5),
        "ffn_dw_w": nrm(ks[14], (DEPTH, FFN_CONV_K, D_FF), FFN_CONV_K ** -0.5),
        "ffn_dw_b": nrm(ks[15], (DEPTH, D_FF), 0.02),
        "ffn_down": nrm(ks[16], (DEPTH, D_FF, D_MODEL), D_FF ** -0.5),
        "final_g": 1.0 + nrm(ks[17], (D_MODEL,), 0.05),
    }


def reference(x, norm1_g, w_in, conv_dw_w, conv_dw_b, conv_ln_g, conv_ln_b, conv_pw_w,
              conv_pw_b, pool_w, pool_scale, w_out, norm2_g, ffn_up, ffn_dw_w, ffn_dw_b,
              ffn_down, final_g):
    b, s, _ = x.shape
    split_at = np.cumsum([D_CONV, D_CONV, D_POOL, D_ATTN, D_ATTN]).tolist()
    for l in range(DEPTH):
        h = rmsnorm(x, norm1_g[l])
        proj = h @ w_in[l]
        c_a, c_g, p_in, q, k, v = jnp.split(proj, split_at, axis=-1)
        y_conv = conformer_conv(c_a, c_g, conv_dw_w[l], conv_dw_b[l], conv_ln_g[l],
                                conv_ln_b[l], conv_pw_w[l], conv_pw_b[l])
        y_pool = multiscale_pool(p_in, pool_w[l], pool_scale[l])
        heads = lambda t: t.reshape(b, s, N_HEADS, HEAD_DIM).transpose(0, 2, 1, 3)
        y_attn = stick_breaking_attention(heads(q), heads(k), heads(v))
        y_attn = y_attn.transpose(0, 2, 1, 3).reshape(b, s, D_ATTN)
        mix = jnp.concatenate([y_conv, y_pool, y_attn], axis=-1)
        x = x + mix @ w_out[l]
        h2 = rmsnorm(x, norm2_g[l])
        val, gate = jnp.split(h2 @ ffn_up[l], 2, axis=-1)
        gate = jax.nn.gelu(causal_dwconv(gate, ffn_dw_w[l], ffn_dw_b[l]), approximate=False)
        x = x + (val * gate) @ ffn_down[l]
    return rmsnorm(x, final_g)
```

```python
import functools

import jax
import jax.numpy as jnp
import numpy as np
from jax import lax
from jax.experimental import pallas as pl
from jax.experimental.pallas import tpu as pltpu

F32 = jnp.float32
BF16 = jnp.bfloat16

D_MODEL = 1024
D_CONV = 256
D_POOL = 256
D_ATTN = 512
N_HEADS = 8
HEAD_DIM = 64
CONV_K = 31
POOL_WINDOWS = (2, 4, 8, 16)
POOL_GROUP = 64
D_FF = 2816
FFN_CONV_K = 3
EPS = 1e-6

LANES = 128
SUBLANES = 8
HALO = 32
ATTN_BLOCK = 128
VMEM_LIMIT = 56 * 1024 * 1024


def _rmsnorm(x, g):
    ms = jnp.mean(x * x, axis=-1, keepdims=True)
    return (x * lax.rsqrt(ms + EPS)) * g


def _inproj_kernel(x_ref, g_ref, w_ref, u_ref, p_ref, q_ref, k_ref, v_ref):
    h = _rmsnorm(x_ref[...], g_ref[...]).astype(BF16)

    def proj(lo, hi):
        return jnp.dot(h, w_ref[:, lo:hi], preferred_element_type=F32)

    o = 0
    a = proj(o, o + D_CONV); o += D_CONV
    gate = proj(o, o + D_CONV); o += D_CONV
    u_ref[...] = a * jax.nn.sigmoid(gate)
    p_ref[...] = proj(o, o + D_POOL); o += D_POOL
    q_ref[...] = (proj(o, o + D_ATTN) * (HEAD_DIM ** -0.5)).astype(BF16); o += D_ATTN
    k_ref[...] = proj(o, o + D_ATTN).astype(BF16); o += D_ATTN
    v_ref[...] = proj(o, o + D_ATTN).astype(BF16)


def _inproj(x2, g, w, tm):
    t = x2.shape[0]
    d_in = w.shape[1]
    row = lambda i: (i, 0)
    const = lambda i: (0, 0)
    return pl.pallas_call(
        _inproj_kernel,
        grid=(t // tm,),
        in_specs=[
            pl.BlockSpec((tm, D_MODEL), row),
            pl.BlockSpec((1, D_MODEL), const),
            pl.BlockSpec((D_MODEL, d_in), const),
        ],
        out_specs=[
            pl.BlockSpec((tm, D_CONV), row),
            pl.BlockSpec((tm, D_POOL), row),
            pl.BlockSpec((tm, D_ATTN), row),
            pl.BlockSpec((tm, D_ATTN), row),
            pl.BlockSpec((tm, D_ATTN), row),
        ],
        out_shape=[
            jax.ShapeDtypeStruct((t, D_CONV), F32),
            jax.ShapeDtypeStruct((t, D_POOL), F32),
            jax.ShapeDtypeStruct((t, D_ATTN), BF16),
            jax.ShapeDtypeStruct((t, D_ATTN), BF16),
            jax.ShapeDtypeStruct((t, D_ATTN), BF16),
        ],
        compiler_params=pltpu.CompilerParams(
            dimension_semantics=("arbitrary",), vmem_limit_bytes=VMEM_LIMIT),
        name="inproj",
    )(x2, g, w)


def _attn_kernel(q_ref, k_ref, v_ref, tri_ref, o_ref, acc_ref, carry_ref):
    i = pl.program_id(2)
    bq = ATTN_BLOCK
    lane = lax.broadcasted_iota(jnp.int32, (bq, LANES), 1)
    head_a = lane < HEAD_DIM
    q = q_ref[...]
    zero = jnp.zeros_like(q)
    q_heads = (jnp.where(head_a, q, zero), jnp.where(head_a, zero, q))
    row = lax.broadcasted_iota(jnp.int32, (bq, bq), 0)
    col = lax.broadcasted_iota(jnp.int32, (bq, bq), 1)
    causal = col < row

    acc_ref[...] = jnp.zeros_like(acc_ref)
    carry_ref[...] = jnp.zeros_like(carry_ref)

    def block(j, diag):
        start = pl.multiple_of(j * bq, bq)
        kj = k_ref[pl.ds(start, bq), :]
        vj = v_ref[pl.ds(start, bq), :]
        zv = jnp.zeros_like(vj)
        v_heads = (jnp.where(head_a, vj, zv), jnp.where(head_a, zv, vj))
        for h in range(2):
            z = lax.dot_general(q_heads[h], kj, (((1,), (1,)), ((), ())),
                                preferred_element_type=F32)
            sp = jnp.maximum(z, 0.0) + jnp.log1p(jnp.exp(-jnp.abs(z)))
            spm = jnp.where(causal, sp, 0.0) if diag else sp
            hi = spm.astype(BF16)
            lo = (spm - hi.astype(F32)).astype(BF16)
            cs = jnp.dot(jnp.concatenate([hi, lo], axis=1), tri_ref[...],
                         preferred_element_type=F32)
            arg = z - sp - cs[:, :bq] - carry_ref[h]
            a = jnp.exp(arg)
            if diag:
                a = jnp.where(causal, a, 0.0)
            acc_ref[...] += jnp.dot(a.astype(BF16), v_heads[h], preferred_element_type=F32)
            carry_ref[h] += cs[:, bq:]

    block(i, True)

    def body(jj, c):
        block(i - 1 - jj, False)
        return c

    lax.fori_loop(0, i, body, 0)
    o_ref[...] = acc_ref[...].astype(o_ref.dtype)


def _tri_matrix():
    bq = ATTN_BLOCK
    j = np.arange(2 * bq)[:, None] % bq
    s = np.arange(2 * bq)[None, :]
    m = np.where(s < bq, j > s, True)
    return jnp.asarray(m, dtype=BF16)


def _attention(q, k, v, batch, seq):
    bq = ATTN_BLOCK
    nq = seq // bq
    n_pairs = D_ATTN // LANES
    return pl.pallas_call(
        _attn_kernel,
        grid=(batch, n_pairs, nq),
        in_specs=[
            pl.BlockSpec((bq, LANES), lambda b, hp, i: (b * nq + i, hp)),
            pl.BlockSpec((seq, LANES), lambda b, hp, i: (b, hp)),
            pl.BlockSpec((seq, LANES), lambda b, hp, i: (b, hp)),
            pl.BlockSpec((2 * bq, 2 * bq), lambda b, hp, i: (0, 0)),
        ],
        out_specs=pl.BlockSpec((bq, LANES), lambda b, hp, i: (b * nq + i, hp)),
        out_shape=jax.ShapeDtypeStruct(q.shape, BF16),
        scratch_shapes=[
            pltpu.VMEM((bq, LANES), F32),
            pltpu.VMEM((2, bq, bq), F32),
        ],
        compiler_params=pltpu.CompilerParams(
            dimension_semantics=("arbitrary", "arbitrary", "arbitrary"),
            vmem_limit_bytes=VMEM_LIMIT),
        name="stickbreak_attn",
    )(q, k, v, _tri_matrix())


def _mix_kernel(u_ref, p_ref, ya_ref, x_ref, dww_ref, dwb_ref, lng_ref, lnb_ref, pww_ref,
                pwb_ref, poolw_ref, pools_ref, wout_ref, o_ref, ubuf, pbuf, *, tm):
    s = pl.program_id(1)

    @pl.when(s == 0)
    def _():
        ubuf[0:HALO, :] = jnp.zeros((HALO, D_CONV), F32)
        pbuf[0:HALO, :] = jnp.zeros((HALO, D_POOL), F32)

    ubuf[HALO:HALO + tm, :] = u_ref[...]
    pbuf[HALO:HALO + tm, :] = p_ref[...]

    conv = jnp.broadcast_to(dwb_ref[...], (tm, D_CONV))
    for k in range(CONV_K):
        off = HALO - (CONV_K - 1) + k
        conv = conv + dww_ref[k:k + 1, :] * ubuf[off:off + tm, :]
    mu = jnp.mean(conv, axis=-1, keepdims=True)
    cen = conv - mu
    var = jnp.mean(cen * cen, axis=-1, keepdims=True)
    y = cen * lax.rsqrt(var + EPS) * lng_ref[...] + lnb_ref[...]
    y = y * jax.nn.sigmoid(y)
    y_conv = jnp.dot(y.astype(BF16), pww_ref[...], preferred_element_type=F32) + pwb_ref[...]

    lane = lax.broadcasted_iota(jnp.int32, (tm, D_POOL), 1)
    win = jnp.left_shift(2, lax.shift_right_logical(lane, int(np.log2(POOL_GROUP))))
    pos = s * tm + lax.broadcasted_iota(jnp.int32, (tm, D_POOL), 0)
    count = jnp.minimum(pos + 1, win).astype(F32)
    cur = pbuf[HALO:HALO + tm, :]
    wsum = cur
    for i in range(1, max(POOL_WINDOWS)):
        wsum = wsum + jnp.where(i < win, pbuf[HALO - i:HALO - i + tm, :], 0.0)
    pooled = wsum / count - cur
    y_pool = jnp.dot(pooled.astype(BF16), poolw_ref[...],
                     preferred_element_type=F32) * pools_ref[...]

    mix = jnp.dot(y_conv.astype(BF16), wout_ref[0:D_CONV, :], preferred_element_type=F32)
    mix += jnp.dot(y_pool.astype(BF16), wout_ref[D_CONV:D_CONV + D_POOL, :],
                   preferred_element_type=F32)
    mix += jnp.dot(ya_ref[...], wout_ref[D_CONV + D_POOL:, :], preferred_element_type=F32)
    o_ref[...] = x_ref[...] + mix

    ubuf[0:HALO, :] = ubuf[tm:tm + HALO, :]
    pbuf[0:HALO, :] = pbuf[tm:tm + HALO, :]


def _mixers(u, p, ya, x2, dww, dwb, lng, lnb, pww, pwb, poolw, pools, wout, batch, seq, tm):
    ns = seq // tm
    row = lambda b, s: (b * ns + s, 0)
    const = lambda b, s: (0, 0)
    full = lambda a: pl.BlockSpec(a.shape, const)
    return pl.pallas_call(
        functools.partial(_mix_kernel, tm=tm),
        grid=(batch, ns),
        in_specs=[
            pl.BlockSpec((tm, D_CONV), row),
            pl.BlockSpec((tm, D_POOL), row),
            pl.BlockSpec((tm, D_ATTN), row),
            pl.BlockSpec((tm, D_MODEL), row),
            full(dww), full(dwb), full(lng), full(lnb), full(pww), full(pwb),
            full(poolw), full(pools), full(wout),
        ],
        out_specs=pl.BlockSpec((tm, D_MODEL), row),
        out_shape=jax.ShapeDtypeStruct(x2.shape, F32),
        scratch_shapes=[
            pltpu.VMEM((HALO + tm, D_CONV), F32),
            pltpu.VMEM((HALO + tm, D_POOL), F32),
        ],
        compiler_params=pltpu.CompilerParams(
            dimension_semantics=("arbitrary", "arbitrary"), vmem_limit_bytes=VMEM_LIMIT),
        name="mixers_outproj",
    )(u, p, ya, x2, dww, dwb, lng, lnb, pww, pwb, poolw, pools, wout)


def _shift_rows(g, prev, r):
    rolled = pltpu.roll(g, r, 0)
    prev_r = pltpu.roll(prev, r, 0)
    first = lax.broadcasted_iota(jnp.int32, prev.shape, 0) < r
    top = jnp.where(first, prev_r, rolled[0:SUBLANES, :])
    return jnp.concatenate([top, rolled[SUBLANES:, :]], axis=0)


def _ffn_kernel(x_ref, g2_ref, wv_ref, wg_ref, dww_ref, dwb_ref, wd_ref, fg_ref, o_ref,
                h_ref, acc_ref, carry_ref, *, tm, tf, final_norm):
    s = pl.program_id(1)
    x = x_ref[...]
    h_ref[...] = _rmsnorm(x, g2_ref[...]).astype(BF16)
    acc_ref[...] = jnp.zeros_like(acc_ref)

    @pl.when(s == 0)
    def _():
        carry_ref[...] = jnp.zeros_like(carry_ref)

    for c in range(D_FF // tf):
        cols = slice(c * tf, (c + 1) * tf)
        h = h_ref[...]
        val = jnp.dot(h, wv_ref[:, cols], preferred_element_type=F32)
        g = jnp.dot(h, wg_ref[:, cols], preferred_element_type=F32)
        prev = carry_ref[:, cols]
        carry_ref[:, cols] = g[tm - SUBLANES:, :]
        conv = (dww_ref[0:1, cols] * _shift_rows(g, prev, 2)
                + dww_ref[1:2, cols] * _shift_rows(g, prev, 1)
                + dww_ref[2:3, cols] * g) + dwb_ref[:, cols]
        gate = 0.5 * conv * (1.0 + lax.erf(conv * np.float32(np.sqrt(0.5))))
        acc_ref[...] += jnp.dot((val * gate).astype(BF16), wd_ref[cols, :],
                                preferred_element_type=F32)

    y = x + acc_ref[...]
    if final_norm:
        y = _rmsnorm(y, fg_ref[...])
    o_ref[...] = y


def _ffn(x2, g2, wv, wg, dww, dwb, wd, fg, batch, seq, tm, tf, final_norm):
    ns = seq // tm
    row = lambda b, s: (b * ns + s, 0)
    const = lambda b, s: (0, 0)
    full = lambda a: pl.BlockSpec(a.shape, const)
    return pl.pallas_call(
        functools.partial(_ffn_kernel, tm=tm, tf=tf, final_norm=final_norm),
        grid=(batch, ns),
        in_specs=[pl.BlockSpec((tm, D_MODEL), row), full(g2), full(wv), full(wg), full(dww),
                  full(dwb), full(wd), full(fg)],
        out_specs=pl.BlockSpec((tm, D_MODEL), row),
        out_shape=jax.ShapeDtypeStruct(x2.shape, F32),
        scratch_shapes=[
            pltpu.VMEM((tm, D_MODEL), BF16),
            pltpu.VMEM((tm, D_MODEL), F32),
            pltpu.VMEM((SUBLANES, D_FF), F32),
        ],
        compiler_params=pltpu.CompilerParams(
            dimension_semantics=("arbitrary", "arbitrary"), vmem_limit_bytes=VMEM_LIMIT),
        name="conv_ffn",
    )(x2, g2, wv, wg, dww, dwb, wd, fg)


def _block_diag(w):
    g, c, _ = w.shape
    eye = jnp.eye(g, dtype=w.dtype)
    return (eye[:, None, :, None] * w[:, :, None, :]).reshape(g * c, g * c)


def kernel(x, norm1_g, w_in, conv_dw_w, conv_dw_b, conv_ln_g, conv_ln_b, conv_pw_w, conv_pw_b,
           pool_w, pool_scale, w_out, norm2_g, ffn_up, ffn_dw_w, ffn_dw_b, ffn_down, final_g):
    batch, seq, d = x.shape
    depth = w_in.shape[0]
    x2 = x.reshape(batch * seq, d)
    vec = lambda a: a.reshape(1, -1)
    for l in range(depth):
        u, p, q, k, v = _inproj(x2, vec(norm1_g[l]), w_in[l].astype(BF16), tm=512)
        ya = _attention(q, k, v, batch, seq)
        x2 = _mixers(u, p, ya, x2, conv_dw_w[l], vec(conv_dw_b[l]), vec(conv_ln_g[l]),
                     vec(conv_ln_b[l]), conv_pw_w[l].astype(BF16), vec(conv_pw_b[l]),
                     _block_diag(pool_w[l]).astype(BF16), vec(pool_scale[l]),
                     w_out[l].astype(BF16), batch, seq, tm=512)
        up = ffn_up[l].astype(BF16)
        x2 = _ffn(x2, vec(norm2_g[l]), up[:, :D_FF], up[:, D_FF:], ffn_dw_w[l],
                  vec(ffn_dw_b[l]), ffn_down[l].astype(BF16), vec(final_g), batch, seq,
                  tm=512, tf=256, final_norm=(l == depth - 1))
    return x2.reshape(batch, seq, d)
```

```python
import functools

import jax
import jax.numpy as jnp
import numpy as np
from jax import lax
from jax.experimental import pallas as pl
from jax.experimental.pallas import tpu as pltpu

F32 = jnp.float32
BF16 = jnp.bfloat16

D_MODEL = 1024
D_CONV = 256
D_POOL = 256
D_ATTN = 512
N_HEADS = 8
HEAD_DIM = 64
CONV_K = 31
POOL_WINDOWS = (2, 4, 8, 16)
POOL_GROUP = 64
D_FF = 2816
FFN_CONV_K = 3
EPS = 1e-6

LANES = 128
SUBLANES = 8
HALO = 32
ATTN_BLOCK = 128
ATTN_TQ = 256
ATTN_TK = 256
MASKED_LOGIT = -1e30
LOG2E = float(np.log2(np.e))
VMEM_LIMIT = 56 * 1024 * 1024


def _rmsnorm(x, g):
    ms = jnp.mean(x * x, axis=-1, keepdims=True)
    return (x * lax.rsqrt(ms + EPS)) * g


def _inproj_kernel(x_ref, g_ref, w_ref, u_ref, p_ref, q_ref, k_ref, v_ref):
    h = _rmsnorm(x_ref[...], g_ref[...]).astype(BF16)

    def proj(lo, hi):
        return jnp.dot(h, w_ref[:, lo:hi], preferred_element_type=F32)

    o = 0
    a = proj(o, o + D_CONV); o += D_CONV
    gate = proj(o, o + D_CONV); o += D_CONV
    u_ref[...] = a * jax.nn.sigmoid(gate)
    p_ref[...] = proj(o, o + D_POOL); o += D_POOL
    q_ref[...] = (proj(o, o + D_ATTN) * (HEAD_DIM ** -0.5)).astype(BF16); o += D_ATTN
    k_ref[...] = proj(o, o + D_ATTN).astype(BF16); o += D_ATTN
    v_ref[...] = proj(o, o + D_ATTN).astype(BF16)


def _inproj(x2, g, w, tm):
    t = x2.shape[0]
    d_in = w.shape[1]
    row = lambda i: (i, 0)
    const = lambda i: (0, 0)
    return pl.pallas_call(
        _inproj_kernel,
        grid=(t // tm,),
        in_specs=[
            pl.BlockSpec((tm, D_MODEL), row),
            pl.BlockSpec((1, D_MODEL), const),
            pl.BlockSpec((D_MODEL, d_in), const),
        ],
        out_specs=[
            pl.BlockSpec((tm, D_CONV), row),
            pl.BlockSpec((tm, D_POOL), row),
            pl.BlockSpec((tm, D_ATTN), row),
            pl.BlockSpec((tm, D_ATTN), row),
            pl.BlockSpec((tm, D_ATTN), row),
        ],
        out_shape=[
            jax.ShapeDtypeStruct((t, D_CONV), F32),
            jax.ShapeDtypeStruct((t, D_POOL), F32),
            jax.ShapeDtypeStruct((t, D_ATTN), BF16),
            jax.ShapeDtypeStruct((t, D_ATTN), BF16),
            jax.ShapeDtypeStruct((t, D_ATTN), BF16),
        ],
        compiler_params=pltpu.CompilerParams(
            dimension_semantics=("arbitrary",), vmem_limit_bytes=VMEM_LIMIT),
        name="inproj",
    )(x2, g, w)


def _attn_kernel(q_ref, k_ref, v_ref, tri_ref, keep_ref, off_ref, o_ref, qm_ref, va_ref, vb_ref,
                 z_ref, zs_ref, cs_ref, a_ref, carry_ref, acc_ref, *, seq):
    tq, tk, blk = ATTN_TQ, ATTN_TK, ATTN_BLOCK
    rows = 2 * tq
    nkb = tk // blk
    nq = seq // tq
    items = [(i, c) for i in range(nq) for c in range(i, -1, -1)]
    n_stages = 4

    head_a = lax.broadcasted_iota(jnp.int32, (tq, LANES), 1) < HEAD_DIM
    for i in range(nq):
        q = q_ref[i * tq:(i + 1) * tq, :]
        v = v_ref[i * tq:(i + 1) * tq, :]
        zero = jnp.zeros_like(q)
        qm_ref[i, 0:tq, :] = jnp.where(head_a, q, zero)
        qm_ref[i, tq:rows, :] = jnp.where(head_a, zero, q)
        va_ref[i * tq:(i + 1) * tq, :] = jnp.where(head_a, v, zero)
        vb_ref[i * tq:(i + 1) * tq, :] = jnp.where(head_a, zero, v)
    carry_ref[...] = jnp.zeros_like(carry_ref)
    acc_ref[...] = jnp.zeros_like(acc_ref)

    def block_rows(idx, size):
        return pl.ds(pl.multiple_of(idx * size, size), size)

    def stage_a(i, c):
        z_ref[...] = lax.dot_general(qm_ref[i], k_ref[block_rows(c, tk), :],
                                     (((1,), (1,)), ((), ())), preferred_element_type=F32)

    def stage_b(diag):
        z = z_ref[...]
        sp = jnp.maximum(z, 0.0) + jnp.log(1.0 + jnp.exp2(jnp.abs(z) * -LOG2E))
        zs = z - sp
        if diag:
            sp = sp * keep_ref[...]
            zs = zs * keep_ref[...] + off_ref[...]
        zs_ref[...] = zs
        hi = sp.astype(BF16)
        lo = (sp - hi.astype(F32)).astype(BF16)
        for kb in range(nkb):
            cols = slice(kb * blk, (kb + 1) * blk)
            cs_ref[kb] = jnp.dot(jnp.concatenate([hi[:, cols], lo[:, cols]], axis=1),
                                 tri_ref[...], preferred_element_type=F32)

    def stage_c():
        carry = carry_ref[...]
        for kb in reversed(range(nkb)):
            cols = slice(kb * blk, (kb + 1) * blk)
            arg = zs_ref[:, cols] - cs_ref[kb, :, 0:blk] - carry
            a_ref[:, cols] = jnp.exp(arg).astype(BF16)
            carry = carry + cs_ref[kb, :, blk:2 * blk]
        carry_ref[...] = carry

    def stage_d(c):
        keys = block_rows(c, tk)
        acc_ref[0:tq, :] += jnp.dot(a_ref[0:tq, :], va_ref[keys, :],
                                    preferred_element_type=F32)
        acc_ref[tq:rows, :] += jnp.dot(a_ref[tq:rows, :], vb_ref[keys, :],
                                       preferred_element_type=F32)

    def when(cond, fn):
        if isinstance(cond, bool):
            if cond:
                fn()
        else:
            pl.when(cond)(fn)

    def close_tile_c(c):
        def reset():
            carry_ref[...] = jnp.zeros_like(carry_ref)
        when(c == 0, reset)

    def close_tile_d(i, c):
        def emit():
            o_ref[block_rows(i, tq), :] = (acc_ref[0:tq, :]
                                           + acc_ref[tq:rows, :]).astype(o_ref.dtype)
            acc_ref[...] = jnp.zeros_like(acc_ref)
        when(c == 0, emit)

    def step(item_a, item_b, item_c, item_d, diag_b):
        if item_d is not None:
            stage_d(item_d[1])
        if item_c is not None:
            stage_c()
        if item_b is not None:
            stage_b(diag_b)
        if item_a is not None:
            stage_a(*item_a)

    def close(item_c, item_d):
        if item_c is not None:
            close_tile_c(item_c[1])
        if item_d is not None:
            close_tile_d(*item_d)

    def static_step(n):
        it = [items[n - d] if 0 <= n - d < len(items) else None for d in range(n_stages)]
        step(*it, diag_b=it[1] is not None and it[1][0] == it[1][1])
        close(it[2], it[3])

    for n in range(n_stages - 1):
        static_step(n)

    def advance(i, c):
        last = c == 0
        return jnp.where(last, i + 1, i), jnp.where(last, i + 1, c - 1)

    def body(n, state):
        item_a, item_b, item_c, item_d = state
        lax.cond(item_b[0] == item_b[1],
                 lambda: step(item_a, item_b, item_c, item_d, True),
                 lambda: step(item_a, item_b, item_c, item_d, False))
        close(item_c, item_d)
        return advance(*item_a), item_a, item_b, item_c

    first = n_stages - 1
    init = tuple(tuple(jnp.int32(x) for x in items[first - d]) for d in range(n_stages))
    lax.fori_loop(first, len(items), body, init)

    for n in range(len(items), len(items) + n_stages - 1):
        static_step(n)


def _tri_matrix():
    blk = ATTN_BLOCK
    j = np.arange(2 * blk)[:, None] % blk
    s = np.arange(2 * blk)[None, :]
    m = np.where(s < blk, j > s, True)
    return jnp.asarray(m, dtype=BF16)


def _causal_masks():
    t = (np.arange(2 * ATTN_TQ) % ATTN_TQ)[:, None]
    s = np.arange(ATTN_TK)[None, :]
    keep = (s < t).astype(np.float32)
    return jnp.asarray(keep), jnp.asarray((1.0 - keep) * np.float32(MASKED_LOGIT))


def _attention(q, k, v, batch, seq):
    tq, tk, blk = ATTN_TQ, ATTN_TK, ATTN_BLOCK
    assert tq == tk and seq % tq == 0
    n_pairs = D_ATTN // LANES
    head_pair = pl.BlockSpec((seq, LANES), lambda b, hp: (b, hp))
    const = lambda shape: pl.BlockSpec(shape, lambda b, hp: (0, 0))
    return pl.pallas_call(
        functools.partial(_attn_kernel, seq=seq),
        grid=(batch, n_pairs),
        in_specs=[head_pair, head_pair, head_pair, const((2 * blk, 2 * blk)),
                  const((2 * tq, tk)), const((2 * tq, tk))],
        out_specs=head_pair,
        out_shape=jax.ShapeDtypeStruct(q.shape, BF16),
        scratch_shapes=[
            pltpu.VMEM((seq // tq, 2 * tq, LANES), BF16),
            pltpu.VMEM((seq, LANES), BF16),
            pltpu.VMEM((seq, LANES), BF16),
            pltpu.VMEM((2 * tq, tk), F32),
            pltpu.VMEM((2 * tq, tk), F32),
            pltpu.VMEM((tk // blk, 2 * tq, 2 * blk), F32),
            pltpu.VMEM((2 * tq, tk), BF16),
            pltpu.VMEM((2 * tq, blk), F32),
            pltpu.VMEM((2 * tq, LANES), F32),
        ],
        compiler_params=pltpu.CompilerParams(
            dimension_semantics=("arbitrary", "arbitrary"), vmem_limit_bytes=VMEM_LIMIT),
        name="stickbreak_attn",
    )(q, k, v, _tri_matrix(), *_causal_masks())


def _mix_kernel(u_ref, p_ref, ya_ref, x_ref, dww_ref, dwb_ref, lng_ref, lnb_ref, pww_ref,
                pwb_ref, poolw_ref, pools_ref, wout_ref, o_ref, ubuf, pbuf, *, tm):
    s = pl.program_id(1)

    @pl.when(s == 0)
    def _():
        ubuf[0:HALO, :] = jnp.zeros((HALO, D_CONV), F32)
        pbuf[0:HALO, :] = jnp.zeros((HALO, D_POOL), F32)

    ubuf[HALO:HALO + tm, :] = u_ref[...]
    pbuf[HALO:HALO + tm, :] = p_ref[...]

    conv = jnp.broadcast_to(dwb_ref[...], (tm, D_CONV))
    for k in range(CONV_K):
        off = HALO - (CONV_K - 1) + k
        conv = conv + dww_ref[k:k + 1, :] * ubuf[off:off + tm, :]
    mu = jnp.mean(conv, axis=-1, keepdims=True)
    cen = conv - mu
    var = jnp.mean(cen * cen, axis=-1, keepdims=True)
    y = cen * lax.rsqrt(var + EPS) * lng_ref[...] + lnb_ref[...]
    y = y * jax.nn.sigmoid(y)
    y_conv = jnp.dot(y.astype(BF16), pww_ref[...], preferred_element_type=F32) + pwb_ref[...]

    lane = lax.broadcasted_iota(jnp.int32, (tm, D_POOL), 1)
    win = jnp.left_shift(2, lax.shift_right_logical(lane, int(np.log2(POOL_GROUP))))
    pos = s * tm + lax.broadcasted_iota(jnp.int32, (tm, D_POOL), 0)
    count = jnp.minimum(pos + 1, win).astype(F32)
    cur = pbuf[HALO:HALO + tm, :]
    wsum = cur
    for i in range(1, max(POOL_WINDOWS)):
        wsum = wsum + jnp.where(i < win, pbuf[HALO - i:HALO - i + tm, :], 0.0)
    pooled = wsum / count - cur
    y_pool = jnp.dot(pooled.astype(BF16), poolw_ref[...],
                     preferred_element_type=F32) * pools_ref[...]

    mix = jnp.dot(y_conv.astype(BF16), wout_ref[0:D_CONV, :], preferred_element_type=F32)
    mix += jnp.dot(y_pool.astype(BF16), wout_ref[D_CONV:D_CONV + D_POOL, :],
                   preferred_element_type=F32)
    mix += jnp.dot(ya_ref[...], wout_ref[D_CONV + D_POOL:, :], preferred_element_type=F32)
    o_ref[...] = x_ref[...] + mix

    ubuf[0:HALO, :] = ubuf[tm:tm + HALO, :]
    pbuf[0:HALO, :] = pbuf[tm:tm + HALO, :]


def _mixers(u, p, ya, x2, dww, dwb, lng, lnb, pww, pwb, poolw, pools, wout, batch, seq, tm):
    ns = seq // tm
    row = lambda b, s: (b * ns + s, 0)
    const = lambda b, s: (0, 0)
    full = lambda a: pl.BlockSpec(a.shape, const)
    return pl.pallas_call(
        functools.partial(_mix_kernel, tm=tm),
        grid=(batch, ns),
        in_specs=[
            pl.BlockSpec((tm, D_CONV), row),
            pl.BlockSpec((tm, D_POOL), row),
            pl.BlockSpec((tm, D_ATTN), row),
            pl.BlockSpec((tm, D_MODEL), row),
            full(dww), full(dwb), full(lng), full(lnb), full(pww), full(pwb),
            full(poolw), full(pools), full(wout),
        ],
        out_specs=pl.BlockSpec((tm, D_MODEL), row),
        out_shape=jax.ShapeDtypeStruct(x2.shape, F32),
        scratch_shapes=[
            pltpu.VMEM((HALO + tm, D_CONV), F32),
            pltpu.VMEM((HALO + tm, D_POOL), F32),
        ],
        compiler_params=pltpu.CompilerParams(
            dimension_semantics=("arbitrary", "arbitrary"), vmem_limit_bytes=VMEM_LIMIT),
        name="mixers_outproj",
    )(u, p, ya, x2, dww, dwb, lng, lnb, pww, pwb, poolw, pools, wout)


def _shift_rows(g, prev, r):
    rolled = pltpu.roll(g, r, 0)
    prev_r = pltpu.roll(prev, r, 0)
    first = lax.broadcasted_iota(jnp.int32, prev.shape, 0) < r
    top = jnp.where(first, prev_r, rolled[0:SUBLANES, :])
    return jnp.concatenate([top, rolled[SUBLANES:, :]], axis=0)


def _ffn_kernel(x_ref, g2_ref, wv_ref, wg_ref, dww_ref, dwb_ref, wd_ref, fg_ref, o_ref,
                h_ref, prod_ref, carry_ref, *, tm, tf, final_norm):
    s = pl.program_id(1)
    x = x_ref[...]
    h_ref[...] = _rmsnorm(x, g2_ref[...]).astype(BF16)

    @pl.when(s == 0)
    def _():
        carry_ref[...] = jnp.zeros_like(carry_ref)

    for c in range(D_FF // tf):
        cols = slice(c * tf, (c + 1) * tf)
        h = h_ref[...]
        val = jnp.dot(h, wv_ref[:, cols], preferred_element_type=F32)
        g = jnp.dot(h, wg_ref[:, cols], preferred_element_type=F32)
        prev = carry_ref[:, cols]
        carry_ref[:, cols] = g[tm - SUBLANES:, :]
        conv = (dww_ref[0:1, cols] * _shift_rows(g, prev, 2)
                + dww_ref[1:2, cols] * _shift_rows(g, prev, 1)
                + dww_ref[2:3, cols] * g) + dwb_ref[:, cols]
        gate = 0.5 * conv * (1.0 + lax.erf(conv * np.float32(np.sqrt(0.5))))
        prod_ref[:, cols] = (val * gate).astype(BF16)

    y = x + jnp.dot(prod_ref[...], wd_ref[...], preferred_element_type=F32)
    if final_norm:
        y = _rmsnorm(y, fg_ref[...])
    o_ref[...] = y


def _ffn(x2, g2, wv, wg, dww, dwb, wd, fg, batch, seq, tm, tf, final_norm):
    ns = seq // tm
    row = lambda b, s: (b * ns + s, 0)
    const = lambda b, s: (0, 0)
    full = lambda a: pl.BlockSpec(a.shape, const)
    return pl.pallas_call(
        functools.partial(_ffn_kernel, tm=tm, tf=tf, final_norm=final_norm),
        grid=(batch, ns),
        in_specs=[pl.BlockSpec((tm, D_MODEL), row), full(g2), full(wv), full(wg), full(dww),
                  full(dwb), full(wd), full(fg)],
        out_specs=pl.BlockSpec((tm, D_MODEL), row),
        out_shape=jax.ShapeDtypeStruct(x2.shape, F32),
        scratch_shapes=[
            pltpu.VMEM((tm, D_MODEL), BF16),
            pltpu.VMEM((tm, D_FF), BF16),
            pltpu.VMEM((SUBLANES, D_FF), F32),
        ],
        compiler_params=pltpu.CompilerParams(
            dimension_semantics=("arbitrary", "arbitrary"), vmem_limit_bytes=VMEM_LIMIT),
        name="conv_ffn",
    )(x2, g2, wv, wg, dww, dwb, wd, fg)


def _block_diag(w):
    g, c, _ = w.shape
    eye = jnp.eye(g, dtype=w.dtype)
    return (eye[:, None, :, None] * w[:, :, None, :]).reshape(g * c, g * c)


def kernel(x, norm1_g, w_in, conv_dw_w, conv_dw_b, conv_ln_g, conv_ln_b, conv_pw_w, conv_pw_b,
           pool_w, pool_scale, w_out, norm2_g, ffn_up, ffn_dw_w, ffn_dw_b, ffn_down, final_g):
    batch, seq, d = x.shape
    depth = w_in.shape[0]
    x2 = x.reshape(batch * seq, d)
    vec = lambda a: a.reshape(1, -1)
    for l in range(depth):
        u, p, q, k, v = _inproj(x2, vec(norm1_g[l]), w_in[l].astype(BF16), tm=512)
        ya = _attention(q, k, v, batch, seq)
        x2 = _mixers(u, p, ya, x2, conv_dw_w[l], vec(conv_dw_b[l]), vec(conv_ln_g[l]),
                     vec(conv_ln_b[l]), conv_pw_w[l].astype(BF16), vec(conv_pw_b[l]),
                     _block_diag(pool_w[l]).astype(BF16), vec(pool_scale[l]),
                     w_out[l].astype(BF16), batch, seq, tm=512)
        up = ffn_up[l].astype(BF16)
        x2 = _ffn(x2, vec(norm2_g[l]), up[:, :D_FF], up[:, D_FF:], ffn_dw_w[l],
                  vec(ffn_dw_b[l]), ffn_down[l].astype(BF16), vec(final_g), batch, seq,
                  tm=512, tf=256, final_norm=(l == depth - 1))
    return x2.reshape(batch, seq, d)
```

```python
import functools

import jax
import jax.numpy as jnp
import numpy as np
from jax import lax
from jax.experimental import pallas as pl
from jax.experimental.pallas import tpu as pltpu

F32 = jnp.float32
BF16 = jnp.bfloat16

D_MODEL = 1024
D_CONV = 256
D_POOL = 256
D_ATTN = 512
N_HEADS = 8
HEAD_DIM = 64
CONV_K = 31
POOL_WINDOWS = (2, 4, 8, 16)
POOL_GROUP = 64
D_FF = 2816
FFN_CONV_K = 3
EPS = 1e-6

LANES = 128
SUBLANES = 8
HALO = 32
ATTN_TQ = 256
ATTN_TK = 256
MASKED_LOGIT = -1e30
LOG2E = float(np.log2(np.e))
VMEM_LIMIT = 56 * 1024 * 1024


def _rmsnorm(x, g):
    ms = jnp.mean(x * x, axis=-1, keepdims=True)
    return (x * lax.rsqrt(ms + EPS)) * g


def _inproj_kernel(x_ref, g_ref, w_ref, u_ref, p_ref, q_ref, k_ref, v_ref):
    h = _rmsnorm(x_ref[...], g_ref[...]).astype(BF16)

    def proj(lo, hi):
        return jnp.dot(h, w_ref[:, lo:hi], preferred_element_type=F32)

    o = 0
    a = proj(o, o + D_CONV); o += D_CONV
    gate = proj(o, o + D_CONV); o += D_CONV
    u_ref[...] = a * jax.nn.sigmoid(gate)
    p_ref[...] = proj(o, o + D_POOL); o += D_POOL
    q_ref[...] = (proj(o, o + D_ATTN) * (HEAD_DIM ** -0.5)).astype(BF16); o += D_ATTN
    k_ref[...] = proj(o, o + D_ATTN).astype(BF16); o += D_ATTN
    v_ref[...] = proj(o, o + D_ATTN).astype(BF16)


def _inproj(x2, g, w, tm):
    t = x2.shape[0]
    d_in = w.shape[1]
    row = lambda i: (i, 0)
    const = lambda i: (0, 0)
    return pl.pallas_call(
        _inproj_kernel,
        grid=(t // tm,),
        in_specs=[
            pl.BlockSpec((tm, D_MODEL), row),
            pl.BlockSpec((1, D_MODEL), const),
            pl.BlockSpec((D_MODEL, d_in), const),
        ],
        out_specs=[
            pl.BlockSpec((tm, D_CONV), row),
            pl.BlockSpec((tm, D_POOL), row),
            pl.BlockSpec((tm, D_ATTN), row),
            pl.BlockSpec((tm, D_ATTN), row),
            pl.BlockSpec((tm, D_ATTN), row),
        ],
        out_shape=[
            jax.ShapeDtypeStruct((t, D_CONV), F32),
            jax.ShapeDtypeStruct((t, D_POOL), F32),
            jax.ShapeDtypeStruct((t, D_ATTN), BF16),
            jax.ShapeDtypeStruct((t, D_ATTN), BF16),
            jax.ShapeDtypeStruct((t, D_ATTN), BF16),
        ],
        compiler_params=pltpu.CompilerParams(
            dimension_semantics=("arbitrary",), vmem_limit_bytes=VMEM_LIMIT),
        name="inproj",
    )(x2, g, w)


def _attn_kernel(q_ref, k_ref, v_ref, tri_ref, keep_ref, off_ref, o_ref, qm_ref, vab_ref, z0_ref,
                 z1_ref, zs0_ref, zs1_ref, h0_ref, h1_ref, cs0_ref, cs1_ref, a_ref, carry_ref,
                 acc_ref, *, seq):
    tq, tk = ATTN_TQ, ATTN_TK
    rows = 2 * tq
    nq = seq // tq
    items = [(i, c) for i in range(nq) for c in range(i, -1, -1)]
    n_stages = 5
    z_refs, zs_refs = (z0_ref, z1_ref), (zs0_ref, zs1_ref)
    h_refs, cs_refs = (h0_ref, h1_ref), (cs0_ref, cs1_ref)

    head_a = lax.broadcasted_iota(jnp.int32, (tq, LANES), 1) < HEAD_DIM
    for i in range(nq):
        q = q_ref[i * tq:(i + 1) * tq, :]
        v = v_ref[i * tq:(i + 1) * tq, :]
        zero = jnp.zeros_like(q)
        qm_ref[i, 0:tq, :] = jnp.where(head_a, q, zero)
        qm_ref[i, tq:rows, :] = jnp.where(head_a, zero, q)
        vab_ref[i, 0:tk, :] = jnp.where(head_a, v, zero)
        vab_ref[i, tk:2 * tk, :] = jnp.where(head_a, zero, v)
    carry_ref[...] = jnp.zeros_like(carry_ref)
    acc_ref[...] = jnp.zeros_like(acc_ref)

    def block_rows(idx, size):
        return pl.ds(pl.multiple_of(idx * size, size), size)

    def stage_a(i, c, z_out):
        z_out[...] = lax.dot_general(qm_ref[i], k_ref[block_rows(c, tk), :],
                                     (((1,), (1,)), ((), ())), preferred_element_type=F32)

    def stage_b1(diag, z_in, zs_out, h_out):
        z = z_in[...]
        sp = jnp.maximum(z, 0.0) + jnp.log(1.0 + jnp.exp2(jnp.abs(z) * -LOG2E))
        zs = z - sp
        if diag:
            sp = sp * keep_ref[...]
            zs = zs * keep_ref[...] + off_ref[...]
        zs_out[...] = zs
        h_out[...] = sp.astype(BF16)

    def stage_b2(h_in, cs_out):
        carry = carry_ref[...]
        cs = jnp.dot(h_in[...], tri_ref[...], preferred_element_type=F32)
        last = cs[:, tk - LANES:tk]
        total = jnp.broadcast_to(last[:, LANES - 1:LANES], (rows, LANES))
        is_total = lax.broadcasted_iota(jnp.int32, (rows, LANES), 1) == LANES - 1
        for lo in range(0, tk - LANES, LANES):
            cs_out[:, lo:lo + LANES] = cs[:, lo:lo + LANES] + carry
        cs_out[:, tk - LANES:tk] = jnp.where(is_total, 0.0, last) + carry
        carry_ref[...] = carry + total

    def stage_c(zs_in, cs_in):
        a_ref[...] = jnp.exp(zs_in[...] - cs_in[...]).astype(BF16)

    def stage_d(c):
        a = jnp.concatenate([a_ref[0:tq, :], a_ref[tq:rows, :]], axis=1)
        acc_ref[...] += jnp.dot(a, vab_ref[c], preferred_element_type=F32)

    def at_tile_end(c, fn):
        if isinstance(c, int):
            if c == 0:
                fn()
        else:
            pl.when(c == 0)(fn)

    def reset_carry():
        carry_ref[...] = jnp.zeros_like(carry_ref)

    def step(item_a, item_b1, item_b2, item_c, item_d, diag_b1, parity):
        if item_b2 is not None:
            stage_b2(h_refs[1 - parity], cs_refs[parity])
        if item_d is not None:
            stage_d(item_d[1])
        if item_a is not None:
            stage_a(*item_a, z_refs[parity])
        if item_c is not None:
            stage_c(zs_refs[parity], cs_refs[1 - parity])
        if item_b1 is not None:
            stage_b1(diag_b1, z_refs[1 - parity], zs_refs[parity], h_refs[parity])
        if item_b2 is not None:
            at_tile_end(item_b2[1], reset_carry)
        if item_d is not None:
            def emit():
                o_ref[block_rows(item_d[0], tq), :] = acc_ref[...].astype(o_ref.dtype)
                acc_ref[...] = jnp.zeros_like(acc_ref)
            at_tile_end(item_d[1], emit)

    def static_step(n):
        it = [items[n - d] if 0 <= n - d < len(items) else None for d in range(n_stages)]
        step(*it, diag_b1=it[1] is not None and it[1][0] == it[1][1], parity=n % 2)

    def advance(i, c):
        last = c == 0
        return jnp.where(last, i + 1, i), jnp.where(last, i + 1, c - 1)

    def dynamic_step(state, parity):
        diag_b1 = state[1][0] == state[1][1]
        lax.cond(diag_b1, lambda: step(*state, True, parity), lambda: step(*state, False, parity))
        return (advance(*state[0]),) + state[:-1]

    first = n_stages - 1
    n_pairs = (len(items) - first) // 2
    for n in range(first):
        static_step(n)

    def pair(_, state):
        return dynamic_step(dynamic_step(state, first % 2), (first + 1) % 2)

    init = tuple(tuple(jnp.int32(x) for x in items[first - d]) for d in range(n_stages))
    lax.fori_loop(0, n_pairs, pair, init)

    for n in range(first + 2 * n_pairs, len(items) + n_stages - 1):
        static_step(n)


def _tri_matrix():
    j = np.arange(ATTN_TK)[:, None]
    s = np.arange(ATTN_TK)[None, :]
    return jnp.asarray((j > s) | (s == ATTN_TK - 1), dtype=BF16)


def _causal_masks():
    t = (np.arange(2 * ATTN_TQ) % ATTN_TQ)[:, None]
    s = np.arange(ATTN_TK)[None, :]
    keep = (s < t).astype(np.float32)
    return jnp.asarray(keep), jnp.asarray((1.0 - keep) * np.float32(MASKED_LOGIT))


def _attention(q, k, v, batch, seq):
    tq, tk = ATTN_TQ, ATTN_TK
    assert tq == tk and seq % tq == 0
    n_pairs = D_ATTN // LANES
    head_pair = pl.BlockSpec((seq, LANES), lambda b, hp: (b, hp))
    const = lambda shape: pl.BlockSpec(shape, lambda b, hp: (0, 0))
    return pl.pallas_call(
        functools.partial(_attn_kernel, seq=seq),
        grid=(batch, n_pairs),
        in_specs=[head_pair, head_pair, head_pair, const((tk, tk)),
                  const((2 * tq, tk)), const((2 * tq, tk))],
        out_specs=head_pair,
        out_shape=jax.ShapeDtypeStruct(q.shape, BF16),
        scratch_shapes=[
            pltpu.VMEM((seq // tq, 2 * tq, LANES), BF16),
            pltpu.VMEM((seq // tk, 2 * tk, LANES), BF16),
            pltpu.VMEM((2 * tq, tk), F32),
            pltpu.VMEM((2 * tq, tk), F32),
            pltpu.VMEM((2 * tq, tk), F32),
            pltpu.VMEM((2 * tq, tk), F32),
            pltpu.VMEM((2 * tq, tk), BF16),
            pltpu.VMEM((2 * tq, tk), BF16),
            pltpu.VMEM((2 * tq, tk), F32),
            pltpu.VMEM((2 * tq, tk), F32),
            pltpu.VMEM((2 * tq, tk), BF16),
            pltpu.VMEM((2 * tq, LANES), F32),
            pltpu.VMEM((tq, LANES), F32),
        ],
        compiler_params=pltpu.CompilerParams(
            dimension_semantics=("arbitrary", "arbitrary"), vmem_limit_bytes=VMEM_LIMIT),
        name="stickbreak_attn",
    )(q, k, v, _tri_matrix(), *_causal_masks())


def _mix_kernel(u_ref, p_ref, ya_ref, x_ref, dww_ref, dwb_ref, lng_ref, lnb_ref, pww_ref,
                pwb_ref, poolw_ref, pools_ref, wout_ref, o_ref, ubuf, pbuf, *, tm):
    s = pl.program_id(1)

    @pl.when(s == 0)
    def _():
        ubuf[0:HALO, :] = jnp.zeros((HALO, D_CONV), F32)
        pbuf[0:HALO, :] = jnp.zeros((HALO, D_POOL), F32)

    ubuf[HALO:HALO + tm, :] = u_ref[...]
    pbuf[HALO:HALO + tm, :] = p_ref[...]

    conv = jnp.broadcast_to(dwb_ref[...], (tm, D_CONV))
    for k in range(CONV_K):
        off = HALO - (CONV_K - 1) + k
        conv = conv + dww_ref[k:k + 1, :] * ubuf[off:off + tm, :]
    mu = jnp.mean(conv, axis=-1, keepdims=True)
    cen = conv - mu
    var = jnp.mean(cen * cen, axis=-1, keepdims=True)
    y = cen * lax.rsqrt(var + EPS) * lng_ref[...] + lnb_ref[...]
    y = y * jax.nn.sigmoid(y)
    y_conv = jnp.dot(y.astype(BF16), pww_ref[...], preferred_element_type=F32) + pwb_ref[...]

    lane = lax.broadcasted_iota(jnp.int32, (tm, D_POOL), 1)
    win = jnp.left_shift(2, lax.shift_right_logical(lane, int(np.log2(POOL_GROUP))))
    pos = s * tm + lax.broadcasted_iota(jnp.int32, (tm, D_POOL), 0)
    count = jnp.minimum(pos + 1, win).astype(F32)
    cur = pbuf[HALO:HALO + tm, :]
    wsum = cur
    for i in range(1, max(POOL_WINDOWS)):
        wsum = wsum + jnp.where(i < win, pbuf[HALO - i:HALO - i + tm, :], 0.0)
    pooled = wsum / count - cur
    y_pool = jnp.dot(pooled.astype(BF16), poolw_ref[...],
                     preferred_element_type=F32) * pools_ref[...]

    mix = jnp.dot(y_conv.astype(BF16), wout_ref[0:D_CONV, :], preferred_element_type=F32)
    mix += jnp.dot(y_pool.astype(BF16), wout_ref[D_CONV:D_CONV + D_POOL, :],
                   preferred_element_type=F32)
    mix += jnp.dot(ya_ref[...], wout_ref[D_CONV + D_POOL:, :], preferred_element_type=F32)
    o_ref[...] = x_ref[...] + mix

    ubuf[0:HALO, :] = ubuf[tm:tm + HALO, :]
    pbuf[0:HALO, :] = pbuf[tm:tm + HALO, :]


def _mixers(u, p, ya, x2, dww, dwb, lng, lnb, pww, pwb, poolw, pools, wout, batch, seq, tm):
    ns = seq // tm
    row = lambda b, s: (b * ns + s, 0)
    const = lambda b, s: (0, 0)
    full = lambda a: pl.BlockSpec(a.shape, const)
    return pl.pallas_call(
        functools.partial(_mix_kernel, tm=tm),
        grid=(batch, ns),
        in_specs=[
            pl.BlockSpec((tm, D_CONV), row),
            pl.BlockSpec((tm, D_POOL), row),
            pl.BlockSpec((tm, D_ATTN), row),
            pl.BlockSpec((tm, D_MODEL), row),
            full(dww), full(dwb), full(lng), full(lnb), full(pww), full(pwb),
            full(poolw), full(pools), full(wout),
        ],
        out_specs=pl.BlockSpec((tm, D_MODEL), row),
        out_shape=jax.ShapeDtypeStruct(x2.shape, F32),
        scratch_shapes=[
            pltpu.VMEM((HALO + tm, D_CONV), F32),
            pltpu.VMEM((HALO + tm, D_POOL), F32),
        ],
        compiler_params=pltpu.CompilerParams(
            dimension_semantics=("arbitrary", "arbitrary"), vmem_limit_bytes=VMEM_LIMIT),
        name="mixers_outproj",
    )(u, p, ya, x2, dww, dwb, lng, lnb, pww, pwb, poolw, pools, wout)


def _shift_rows(g, prev, r):
    rolled = pltpu.roll(g, r, 0)
    prev_r = pltpu.roll(prev, r, 0)
    first = lax.broadcasted_iota(jnp.int32, prev.shape, 0) < r
    top = jnp.where(first, prev_r, rolled[0:SUBLANES, :])
    return jnp.concatenate([top, rolled[SUBLANES:, :]], axis=0)


def _ffn_kernel(x_ref, g2_ref, wv_ref, wg_ref, dww_ref, dwb_ref, wd_ref, fg_ref, o_ref,
                h_ref, prod_ref, carry_ref, *, tm, tf, final_norm):
    s = pl.program_id(1)
    x = x_ref[...]
    h_ref[...] = _rmsnorm(x, g2_ref[...]).astype(BF16)

    @pl.when(s == 0)
    def _():
        carry_ref[...] = jnp.zeros_like(carry_ref)

    for c in range(D_FF // tf):
        cols = slice(c * tf, (c + 1) * tf)
        h = h_ref[...]
        val = jnp.dot(h, wv_ref[:, cols], preferred_element_type=F32)
        g = jnp.dot(h, wg_ref[:, cols], preferred_element_type=F32)
        prev = carry_ref[:, cols]
        carry_ref[:, cols] = g[tm - SUBLANES:, :]
        conv = (dww_ref[0:1, cols] * _shift_rows(g, prev, 2)
                + dww_ref[1:2, cols] * _shift_rows(g, prev, 1)
                + dww_ref[2:3, cols] * g) + dwb_ref[:, cols]
        gate = 0.5 * conv * (1.0 + lax.erf(conv * np.float32(np.sqrt(0.5))))
        prod_ref[:, cols] = (val * gate).astype(BF16)

    y = x + jnp.dot(prod_ref[...], wd_ref[...], preferred_element_type=F32)
    if final_norm:
        y = _rmsnorm(y, fg_ref[...])
    o_ref[...] = y


def _ffn(x2, g2, w_up, dww, dwb, wd, fg, batch, seq, tm, tf, final_norm):
    ns = seq // tm
    row = lambda b, s: (b * ns + s, 0)
    const = lambda b, s: (0, 0)
    full = lambda a: pl.BlockSpec(a.shape, const)
    return pl.pallas_call(
        functools.partial(_ffn_kernel, tm=tm, tf=tf, final_norm=final_norm),
        grid=(batch, ns),
        in_specs=[pl.BlockSpec((tm, D_MODEL), row), full(g2),
                  pl.BlockSpec((D_MODEL, D_FF), lambda b, s: (0, 0)),
                  pl.BlockSpec((D_MODEL, D_FF), lambda b, s: (0, 1)),
                  full(dww), full(dwb), full(wd), full(fg)],
        out_specs=pl.BlockSpec((tm, D_MODEL), row),
        out_shape=jax.ShapeDtypeStruct(x2.shape, F32),
        scratch_shapes=[
            pltpu.VMEM((tm, D_MODEL), BF16),
            pltpu.VMEM((tm, D_FF), BF16),
            pltpu.VMEM((SUBLANES, D_FF), F32),
        ],
        compiler_params=pltpu.CompilerParams(
            dimension_semantics=("arbitrary", "arbitrary"), vmem_limit_bytes=VMEM_LIMIT),
        name="conv_ffn",
    )(x2, g2, w_up, w_up, dww, dwb, wd, fg)


def _block_diag(w):
    g, c, _ = w.shape
    eye = jnp.eye(g, dtype=w.dtype)
    return (eye[:, None, :, None] * w[:, :, None, :]).reshape(g * c, g * c)


def kernel(x, norm1_g, w_in, conv_dw_w, conv_dw_b, conv_ln_g, conv_ln_b, conv_pw_w, conv_pw_b,
           pool_w, pool_scale, w_out, norm2_g, ffn_up, ffn_dw_w, ffn_dw_b, ffn_down, final_g):
    batch, seq, d = x.shape
    depth = w_in.shape[0]
    x2 = x.reshape(batch * seq, d)
    vec = lambda a: a.reshape(1, -1)
    for l in range(depth):
        u, p, q, k, v = _inproj(x2, vec(norm1_g[l]), w_in[l].astype(BF16), tm=512)
        ya = _attention(q, k, v, batch, seq)
        x2 = _mixers(u, p, ya, x2, conv_dw_w[l], vec(conv_dw_b[l]), vec(conv_ln_g[l]),
                     vec(conv_ln_b[l]), conv_pw_w[l].astype(BF16), vec(conv_pw_b[l]),
                     _block_diag(pool_w[l]).astype(BF16), vec(pool_scale[l]),
                     w_out[l].astype(BF16), batch, seq, tm=512)
        x2 = _ffn(x2, vec(norm2_g[l]), ffn_up[l].astype(BF16), ffn_dw_w[l],
                  vec(ffn_dw_b[l]), ffn_down[l].astype(BF16), vec(final_g), batch, seq,
                  tm=512, tf=256, final_norm=(l == depth - 1))
    return x2.reshape(batch, seq, d)
```

```python
import functools

import jax
import jax.numpy as jnp
import numpy as np
from jax import lax
from jax.experimental import pallas as pl
from jax.experimental.pallas import tpu as pltpu

F32 = jnp.float32
BF16 = jnp.bfloat16

D_MODEL = 1024
D_CONV = 256
D_POOL = 256
D_ATTN = 512
N_HEADS = 8
HEAD_DIM = 64
CONV_K = 31
POOL_WINDOWS = (2, 4, 8, 16)
POOL_GROUP = 64
D_FF = 2816
FFN_CONV_K = 3
EPS = 1e-6

LANES = 128
SUBLANES = 8
HALO = 32
ATTN_TQ = 256
ATTN_TK = 256
ATTN_UNROLL = 4
MASKED_LOGIT = -1e30
LOG2E = float(np.log2(np.e))
VMEM_LIMIT = 56 * 1024 * 1024


def _rmsnorm(x, g):
    ms = jnp.mean(x * x, axis=-1, keepdims=True)
    return (x * lax.rsqrt(ms + EPS)) * g


def _inproj_kernel(x_ref, g_ref, w_ref, u_ref, p_ref, q_ref, k_ref, v_ref):
    h = _rmsnorm(x_ref[...], g_ref[...]).astype(BF16)

    def proj(lo, hi):
        return jnp.dot(h, w_ref[:, lo:hi], preferred_element_type=F32)

    o = 0
    a = proj(o, o + D_CONV); o += D_CONV
    gate = proj(o, o + D_CONV); o += D_CONV
    u_ref[...] = a * jax.nn.sigmoid(gate)
    p_ref[...] = proj(o, o + D_POOL); o += D_POOL
    q_ref[...] = (proj(o, o + D_ATTN) * (HEAD_DIM ** -0.5)).astype(BF16); o += D_ATTN
    k_ref[...] = proj(o, o + D_ATTN).astype(BF16); o += D_ATTN
    v_ref[...] = proj(o, o + D_ATTN).astype(BF16)


def _inproj(x2, g, w, tm):
    t = x2.shape[0]
    d_in = w.shape[1]
    row = lambda i: (i, 0)
    const = lambda i: (0, 0)
    return pl.pallas_call(
        _inproj_kernel,
        grid=(t // tm,),
        in_specs=[
            pl.BlockSpec((tm, D_MODEL), row),
            pl.BlockSpec((1, D_MODEL), const),
            pl.BlockSpec((D_MODEL, d_in), const),
        ],
        out_specs=[
            pl.BlockSpec((tm, D_CONV), row),
            pl.BlockSpec((tm, D_POOL), row),
            pl.BlockSpec((tm, D_ATTN), row),
            pl.BlockSpec((tm, D_ATTN), row),
            pl.BlockSpec((tm, D_ATTN), row),
        ],
        out_shape=[
            jax.ShapeDtypeStruct((t, D_CONV), F32),
            jax.ShapeDtypeStruct((t, D_POOL), F32),
            jax.ShapeDtypeStruct((t, D_ATTN), BF16),
            jax.ShapeDtypeStruct((t, D_ATTN), BF16),
            jax.ShapeDtypeStruct((t, D_ATTN), BF16),
        ],
        compiler_params=pltpu.CompilerParams(
            dimension_semantics=("arbitrary",), vmem_limit_bytes=VMEM_LIMIT),
        name="inproj",
    )(x2, g, w)


def _attn_kernel(q_ref, k_ref, v_ref, tri_ref, keep_ref, off_ref, o_ref, qm_ref, vab_ref, z_ref,
                 zs_ref, h_ref, cs_ref, a_ref, carry_ref, acc_ref, tile_carry_ref, tile_acc_ref,
                 *, seq):
    tq, tk = ATTN_TQ, ATTN_TK
    rows = 2 * tq
    nq = seq // tq
    off_diag = [(i, c) for i in range(1, nq) for c in range(i - 1, -1, -1)]
    items = [(i, i) for i in range(nq)] + off_diag
    n_stages = 5
    n_slots = ATTN_UNROLL

    head_a = lax.broadcasted_iota(jnp.int32, (tq, LANES), 1) < HEAD_DIM
    for i in range(nq):
        q = q_ref[i * tq:(i + 1) * tq, :]
        v = v_ref[i * tq:(i + 1) * tq, :]
        zero = jnp.zeros_like(q)
        qm_ref[i, 0:tq, :] = jnp.where(head_a, q, zero)
        qm_ref[i, tq:rows, :] = jnp.where(head_a, zero, q)
        vab_ref[i, 0:tk, :] = jnp.where(head_a, v, zero)
        vab_ref[i, tk:2 * tk, :] = jnp.where(head_a, zero, v)

    def block_rows(idx, size):
        return pl.ds(pl.multiple_of(idx * size, size), size)

    def stage_a(i, c, z_out):
        z_out[...] = lax.dot_general(qm_ref[i], k_ref[block_rows(c, tk), :],
                                     (((1,), (1,)), ((), ())), preferred_element_type=F32)

    def stage_b1(diag, z_in, zs_out, h_out):
        z = z_in[...]
        sp = jnp.maximum(z, 0.0) + jnp.log(1.0 + jnp.exp2(jnp.abs(z) * -LOG2E))
        zs = z - sp
        if diag:
            sp = sp * keep_ref[...]
            zs = zs * keep_ref[...] + off_ref[...]
        zs_out[...] = zs
        h_out[...] = sp.astype(BF16)

    def stage_b2(i, c, diag, h_in, cs_out, carry_in, carry_out):
        cs = jnp.dot(h_in[...], tri_ref[...], preferred_element_type=F32)
        last = cs[:, tk - LANES:tk]
        total = jnp.broadcast_to(last[:, LANES - 1:LANES], (rows, LANES))
        is_total = lax.broadcasted_iota(jnp.int32, (rows, LANES), 1) == LANES - 1
        last = jnp.where(is_total, 0.0, last)
        if diag:
            cs_out[:, 0:tk - LANES] = cs[:, 0:tk - LANES]
            cs_out[:, tk - LANES:tk] = last
            tile_carry_ref[i] = total
        else:
            carry = jnp.where(c == i - 1, tile_carry_ref[i], carry_in[...])
            for lo in range(0, tk - LANES, LANES):
                cs_out[:, lo:lo + LANES] = cs[:, lo:lo + LANES] + carry
            cs_out[:, tk - LANES:tk] = last + carry
            carry_out[...] = carry + total

    def stage_c(zs_in, cs_in, a_out):
        a_out[...] = jnp.exp(zs_in[...] - cs_in[...]).astype(BF16)

    def stage_d(i, c, diag, a_in, acc_in, acc_out):
        a = jnp.concatenate([a_in[0:tq, :], a_in[tq:rows, :]], axis=1)
        acc = jnp.dot(a, vab_ref[c], preferred_element_type=F32)
        if diag:
            tile_acc_ref[i] = acc
        else:
            acc += jnp.where(c == i - 1, tile_acc_ref[i], acc_in[...])
            acc_out[...] = acc
        o_ref[block_rows(i, tq), :] = acc.astype(o_ref.dtype)

    def step(item_a, item_b1, item_b2, item_c, item_d, diag_b1, diag_b2, diag_d, slot):
        prev, prev2 = (slot - 1) % n_slots, (slot - 2) % n_slots
        if item_b2 is not None:
            stage_b2(*item_b2, diag_b2, h_ref.at[prev], cs_ref.at[slot],
                     carry_ref.at[prev], carry_ref.at[slot])
        if item_d is not None:
            stage_d(*item_d, diag_d, a_ref.at[prev], acc_ref.at[prev], acc_ref.at[slot])
        if item_a is not None:
            stage_a(*item_a, z_ref.at[slot])
        if item_c is not None:
            stage_c(zs_ref.at[prev2], cs_ref.at[prev], a_ref.at[slot])
        if item_b1 is not None:
            stage_b1(diag_b1, z_ref.at[prev], zs_ref.at[slot], h_ref.at[slot])

    def static_step(n):
        it = [items[n - d] if 0 <= n - d < len(items) else None for d in range(n_stages)]
        diag = [x is not None and x[0] == x[1] for x in it]
        step(*it, diag_b1=diag[1], diag_b2=diag[2], diag_d=diag[4], slot=n % n_slots)

    first = nq + n_stages - 1
    n_loops = (len(items) - first) // ATTN_UNROLL
    assert first % n_slots == 0
    for n in range(first):
        static_step(n)

    def advance(i, c):
        last = c == 0
        return jnp.where(last, i + 1, i), jnp.where(last, i, c - 1)

    def body(_, state):
        for u in range(ATTN_UNROLL):
            step(*state, False, False, False, u)
            state = (advance(*state[0]),) + state[:-1]
        return state

    init = tuple(tuple(jnp.int32(x) for x in items[first - d]) for d in range(n_stages))
    lax.fori_loop(0, n_loops, body, init)

    for n in range(first + ATTN_UNROLL * n_loops, len(items) + n_stages - 1):
        static_step(n)


def _tri_matrix():
    j = np.arange(ATTN_TK)[:, None]
    s = np.arange(ATTN_TK)[None, :]
    return jnp.asarray((j > s) | (s == ATTN_TK - 1), dtype=BF16)


def _causal_masks():
    t = (np.arange(2 * ATTN_TQ) % ATTN_TQ)[:, None]
    s = np.arange(ATTN_TK)[None, :]
    keep = (s < t).astype(np.float32)
    return jnp.asarray(keep), jnp.asarray((1.0 - keep) * np.float32(MASKED_LOGIT))


def _attention(q, k, v, batch, seq):
    tq, tk = ATTN_TQ, ATTN_TK
    assert tq == tk and seq % tq == 0
    n_pairs = D_ATTN // LANES
    head_pair = pl.BlockSpec((seq, LANES), lambda b, hp: (b, hp))
    const = lambda shape: pl.BlockSpec(shape, lambda b, hp: (0, 0))
    return pl.pallas_call(
        functools.partial(_attn_kernel, seq=seq),
        grid=(batch, n_pairs),
        in_specs=[head_pair, head_pair, head_pair, const((tk, tk)),
                  const((2 * tq, tk)), const((2 * tq, tk))],
        out_specs=head_pair,
        out_shape=jax.ShapeDtypeStruct(q.shape, BF16),
        scratch_shapes=[
            pltpu.VMEM((seq // tq, 2 * tq, LANES), BF16),
            pltpu.VMEM((seq // tk, 2 * tk, LANES), BF16),
            pltpu.VMEM((ATTN_UNROLL, 2 * tq, tk), F32),
            pltpu.VMEM((ATTN_UNROLL, 2 * tq, tk), F32),
            pltpu.VMEM((ATTN_UNROLL, 2 * tq, tk), BF16),
            pltpu.VMEM((ATTN_UNROLL, 2 * tq, tk), F32),
            pltpu.VMEM((ATTN_UNROLL, 2 * tq, tk), BF16),
            pltpu.VMEM((ATTN_UNROLL, 2 * tq, LANES), F32),
            pltpu.VMEM((ATTN_UNROLL, tq, LANES), F32),
            pltpu.VMEM((seq // tq, 2 * tq, LANES), F32),
            pltpu.VMEM((seq // tq, tq, LANES), F32),
        ],
        compiler_params=pltpu.CompilerParams(
            dimension_semantics=("arbitrary", "arbitrary"), vmem_limit_bytes=VMEM_LIMIT),
        name="stickbreak_attn",
    )(q, k, v, _tri_matrix(), *_causal_masks())


def _mix_kernel(u_ref, p_ref, ya_ref, x_ref, dww_ref, dwb_ref, lng_ref, lnb_ref, pww_ref,
                pwb_ref, poolw_ref, pools_ref, wout_ref, o_ref, ubuf, pbuf, *, tm):
    s = pl.program_id(1)

    @pl.when(s == 0)
    def _():
        ubuf[0:HALO, :] = jnp.zeros((HALO, D_CONV), F32)
        pbuf[0:HALO, :] = jnp.zeros((HALO, D_POOL), F32)

    ubuf[HALO:HALO + tm, :] = u_ref[...]
    pbuf[HALO:HALO + tm, :] = p_ref[...]

    conv = jnp.broadcast_to(dwb_ref[...], (tm, D_CONV))
    for k in range(CONV_K):
        off = HALO - (CONV_K - 1) + k
        conv = conv + dww_ref[k:k + 1, :] * ubuf[off:off + tm, :]
    mu = jnp.mean(conv, axis=-1, keepdims=True)
    cen = conv - mu
    var = jnp.mean(cen * cen, axis=-1, keepdims=True)
    y = cen * lax.rsqrt(var + EPS) * lng_ref[...] + lnb_ref[...]
    y = y * jax.nn.sigmoid(y)
    y_conv = jnp.dot(y.astype(BF16), pww_ref[...], preferred_element_type=F32) + pwb_ref[...]

    lane = lax.broadcasted_iota(jnp.int32, (tm, D_POOL), 1)
    win = jnp.left_shift(2, lax.shift_right_logical(lane, int(np.log2(POOL_GROUP))))
    pos = s * tm + lax.broadcasted_iota(jnp.int32, (tm, D_POOL), 0)
    count = jnp.minimum(pos + 1, win).astype(F32)
    cur = pbuf[HALO:HALO + tm, :]
    wsum = cur
    for i in range(1, max(POOL_WINDOWS)):
        wsum = wsum + jnp.where(i < win, pbuf[HALO - i:HALO - i + tm, :], 0.0)
    pooled = wsum / count - cur
    y_pool = jnp.dot(pooled.astype(BF16), poolw_ref[...],
                     preferred_element_type=F32) * pools_ref[...]

    mix = jnp.dot(y_conv.astype(BF16), wout_ref[0:D_CONV, :], preferred_element_type=F32)
    mix += jnp.dot(y_pool.astype(BF16), wout_ref[D_CONV:D_CONV + D_POOL, :],
                   preferred_element_type=F32)
    mix += jnp.dot(ya_ref[...], wout_ref[D_CONV + D_POOL:, :], preferred_element_type=F32)
    o_ref[...] = x_ref[...] + mix

    ubuf[0:HALO, :] = ubuf[tm:tm + HALO, :]
    pbuf[0:HALO, :] = pbuf[tm:tm + HALO, :]


def _mixers(u, p, ya, x2, dww, dwb, lng, lnb, pww, pwb, poolw, pools, wout, batch, seq, tm):
    ns = seq // tm
    row = lambda b, s: (b * ns + s, 0)
    const = lambda b, s: (0, 0)
    full = lambda a: pl.BlockSpec(a.shape, const)
    return pl.pallas_call(
        functools.partial(_mix_kernel, tm=tm),
        grid=(batch, ns),
        in_specs=[
            pl.BlockSpec((tm, D_CONV), row),
            pl.BlockSpec((tm, D_POOL), row),
            pl.BlockSpec((tm, D_ATTN), row),
            pl.BlockSpec((tm, D_MODEL), row),
            full(dww), full(dwb), full(lng), full(lnb), full(pww), full(pwb),
            full(poolw), full(pools), full(wout),
        ],
        out_specs=pl.BlockSpec((tm, D_MODEL), row),
        out_shape=jax.ShapeDtypeStruct(x2.shape, F32),
        scratch_shapes=[
            pltpu.VMEM((HALO + tm, D_CONV), F32),
            pltpu.VMEM((HALO + tm, D_POOL), F32),
        ],
        compiler_params=pltpu.CompilerParams(
            dimension_semantics=("arbitrary", "arbitrary"), vmem_limit_bytes=VMEM_LIMIT),
        name="mixers_outproj",
    )(u, p, ya, x2, dww, dwb, lng, lnb, pww, pwb, poolw, pools, wout)


def _shift_rows(g, prev, r):
    rolled = pltpu.roll(g, r, 0)
    prev_r = pltpu.roll(prev, r, 0)
    first = lax.broadcasted_iota(jnp.int32, prev.shape, 0) < r
    top = jnp.where(first, prev_r, rolled[0:SUBLANES, :])
    return jnp.concatenate([top, rolled[SUBLANES:, :]], axis=0)


def _ffn_kernel(x_ref, g2_ref, wv_ref, wg_ref, dww_ref, dwb_ref, wd_ref, fg_ref, o_ref,
                h_ref, prod_ref, carry_ref, *, tm, tf, final_norm):
    s = pl.program_id(1)
    x = x_ref[...]
    h_ref[...] = _rmsnorm(x, g2_ref[...]).astype(BF16)

    @pl.when(s == 0)
    def _():
        carry_ref[...] = jnp.zeros_like(carry_ref)

    for c in range(D_FF // tf):
        cols = slice(c * tf, (c + 1) * tf)
        h = h_ref[...]
        val = jnp.dot(h, wv_ref[:, cols], preferred_element_type=F32)
        g = jnp.dot(h, wg_ref[:, cols], preferred_element_type=F32)
        prev = carry_ref[:, cols]
        carry_ref[:, cols] = g[tm - SUBLANES:, :]
        conv = (dww_ref[0:1, cols] * _shift_rows(g, prev, 2)
                + dww_ref[1:2, cols] * _shift_rows(g, prev, 1)
                + dww_ref[2:3, cols] * g) + dwb_ref[:, cols]
        gate = 0.5 * conv * (1.0 + lax.erf(conv * np.float32(np.sqrt(0.5))))
        prod_ref[:, cols] = (val * gate).astype(BF16)

    y = x + jnp.dot(prod_ref[...], wd_ref[...], preferred_element_type=F32)
    if final_norm:
        y = _rmsnorm(y, fg_ref[...])
    o_ref[...] = y


def _ffn(x2, g2, w_up, dww, dwb, wd, fg, batch, seq, tm, tf, final_norm):
    ns = seq // tm
    row = lambda b, s: (b * ns + s, 0)
    const = lambda b, s: (0, 0)
    full = lambda a: pl.BlockSpec(a.shape, const)
    return pl.pallas_call(
        functools.partial(_ffn_kernel, tm=tm, tf=tf, final_norm=final_norm),
        grid=(batch, ns),
        in_specs=[pl.BlockSpec((tm, D_MODEL), row), full(g2),
                  pl.BlockSpec((D_MODEL, D_FF), lambda b, s: (0, 0)),
                  pl.BlockSpec((D_MODEL, D_FF), lambda b, s: (0, 1)),
                  full(dww), full(dwb), full(wd), full(fg)],
        out_specs=pl.BlockSpec((tm, D_MODEL), row),
        out_shape=jax.ShapeDtypeStruct(x2.shape, F32),
        scratch_shapes=[
            pltpu.VMEM((tm, D_MODEL), BF16),
            pltpu.VMEM((tm, D_FF), BF16),
            pltpu.VMEM((SUBLANES, D_FF), F32),
        ],
        compiler_params=pltpu.CompilerParams(
            dimension_semantics=("arbitrary", "arbitrary"), vmem_limit_bytes=VMEM_LIMIT),
        name="conv_ffn",
    )(x2, g2, w_up, w_up, dww, dwb, wd, fg)


def _block_diag(w):
    g, c, _ = w.shape
    eye = jnp.eye(g, dtype=w.dtype)
    return (eye[:, None, :, None] * w[:, :, None, :]).reshape(g * c, g * c)


def kernel(x, norm1_g, w_in, conv_dw_w, conv_dw_b, conv_ln_g, conv_ln_b, conv_pw_w, conv_pw_b,
           pool_w, pool_scale, w_out, norm2_g, ffn_up, ffn_dw_w, ffn_dw_b, ffn_down, final_g):
    batch, seq, d = x.shape
    depth = w_in.shape[0]
    x2 = x.reshape(batch * seq, d)
    vec = lambda a: a.reshape(1, -1)
    for l in range(depth):
        u, p, q, k, v = _inproj(x2, vec(norm1_g[l]), w_in[l].astype(BF16), tm=512)
        ya = _attention(q, k, v, batch, seq)
        x2 = _mixers(u, p, ya, x2, conv_dw_w[l], vec(conv_dw_b[l]), vec(conv_ln_g[l]),
                     vec(conv_ln_b[l]), conv_pw_w[l].astype(BF16), vec(conv_pw_b[l]),
                     _block_diag(pool_w[l]).astype(BF16), vec(pool_scale[l]),
                     w_out[l].astype(BF16), batch, seq, tm=512)
        x2 = _ffn(x2, vec(norm2_g[l]), ffn_up[l].astype(BF16), ffn_dw_w[l],
                  vec(ffn_dw_b[l]), ffn_down[l].astype(BF16), vec(final_g), batch, seq,
                  tm=512, tf=256, final_norm=(l == depth - 1))
    return x2.reshape(batch, seq, d)
```

```python
import functools

import jax
import jax.numpy as jnp
import numpy as np
from jax import lax
from jax.experimental import pallas as pl
from jax.experimental.pallas import tpu as pltpu

F32 = jnp.float32
BF16 = jnp.bfloat16

D_MODEL = 1024
D_CONV = 256
D_POOL = 256
D_ATTN = 512
N_HEADS = 8
HEAD_DIM = 64
CONV_K = 31
POOL_WINDOWS = (2, 4, 8, 16)
N_POOL_GROUPS = len(POOL_WINDOWS)
POOL_GROUP = 64
D_FF = 2816
FFN_CONV_K = 3
EPS = 1e-6

LANES = 128
SUBLANES = 8
HALO = 32
ATTN_TQ = 256
ATTN_TK = 256
ATTN_UNROLL = 6
MASKED_LOGIT = -1e30
LOG2E = float(np.log2(np.e))
VMEM_LIMIT = 56 * 1024 * 1024


def _rmsnorm(x, g):
    ms = jnp.mean(x * x, axis=-1, keepdims=True)
    return (x * lax.rsqrt(ms + EPS)) * g


def _inproj_kernel(x_ref, g_ref, w_ref, u_ref, p_ref, q_ref, k_ref, v_ref):
    h = _rmsnorm(x_ref[...], g_ref[...]).astype(BF16)

    def proj(lo, hi):
        return jnp.dot(h, w_ref[:, lo:hi], preferred_element_type=F32)

    o = 0
    a = proj(o, o + D_CONV); o += D_CONV
    gate = proj(o, o + D_CONV); o += D_CONV
    u_ref[...] = a * jax.nn.sigmoid(gate)
    p_ref[...] = proj(o, o + D_POOL); o += D_POOL
    q_ref[...] = (proj(o, o + D_ATTN) * (HEAD_DIM ** -0.5)).astype(BF16); o += D_ATTN
    k_ref[...] = proj(o, o + D_ATTN).astype(BF16); o += D_ATTN
    v_ref[...] = proj(o, o + D_ATTN).astype(BF16)


def _inproj(x2, g, w, tm):
    t = x2.shape[0]
    d_in = w.shape[1]
    row = lambda i: (i, 0)
    const = lambda i: (0, 0)
    return pl.pallas_call(
        _inproj_kernel,
        grid=(t // tm,),
        in_specs=[
            pl.BlockSpec((tm, D_MODEL), row),
            pl.BlockSpec((1, D_MODEL), const),
            pl.BlockSpec((D_MODEL, d_in), const),
        ],
        out_specs=[
            pl.BlockSpec((tm, D_CONV), row),
            pl.BlockSpec((tm, D_POOL), row),
            pl.BlockSpec((tm, D_ATTN), row),
            pl.BlockSpec((tm, D_ATTN), row),
            pl.BlockSpec((tm, D_ATTN), row),
        ],
        out_shape=[
            jax.ShapeDtypeStruct((t, D_CONV), F32),
            jax.ShapeDtypeStruct((t, D_POOL), F32),
            jax.ShapeDtypeStruct((t, D_ATTN), BF16),
            jax.ShapeDtypeStruct((t, D_ATTN), BF16),
            jax.ShapeDtypeStruct((t, D_ATTN), BF16),
        ],
        compiler_params=pltpu.CompilerParams(
            dimension_semantics=("arbitrary",), vmem_limit_bytes=VMEM_LIMIT),
        name="inproj",
    )(x2, g, w)


def _attn_kernel(q_ref, k_ref, v_ref, tri_ref, keep_ref, off_ref, o_ref, qm_ref, vab_ref, z_ref,
                 zs_ref, h_ref, cs_ref, a_ref, carry_ref, acc_ref, tile_carry_ref, tile_acc_ref,
                 *, seq):
    tq, tk = ATTN_TQ, ATTN_TK
    rows = 2 * tq
    nq = seq // tq
    off_diag = [(i, c) for i in range(1, nq) for c in range(i - 1, -1, -1)]
    items = [(i, i) for i in range(nq)] + off_diag
    n_stages = 5
    n_slots = ATTN_UNROLL

    head_a = lax.broadcasted_iota(jnp.int32, (tq, LANES), 1) < HEAD_DIM
    for i in range(nq):
        q = q_ref[i * tq:(i + 1) * tq, :]
        v = v_ref[i * tq:(i + 1) * tq, :]
        zero = jnp.zeros_like(q)
        qm_ref[i, 0:tq, :] = jnp.where(head_a, q, zero)
        qm_ref[i, tq:rows, :] = jnp.where(head_a, zero, q)
        vab_ref[i, 0:tk, :] = jnp.where(head_a, v, zero)
        vab_ref[i, tk:2 * tk, :] = jnp.where(head_a, zero, v)

    def block_rows(idx, size):
        return pl.ds(pl.multiple_of(idx * size, size), size)

    def stage_a(i, c, z_out):
        z_out[...] = lax.dot_general(qm_ref[i], k_ref[block_rows(c, tk), :],
                                     (((1,), (1,)), ((), ())), preferred_element_type=F32)

    def stage_b1(diag, z_in, zs_out, h_out):
        z = z_in[...]
        sp = jnp.maximum(z, 0.0) + jnp.log(1.0 + jnp.exp2(jnp.abs(z) * -LOG2E))
        zs = z - sp
        if diag:
            sp = sp * keep_ref[...]
            zs = zs * keep_ref[...] + off_ref[...]
        zs_out[...] = zs
        h_out[...] = sp.astype(BF16)

    def stage_b2(i, c, diag, h_in, cs_out, carry_in, carry_out):
        cs = jnp.dot(h_in[...], tri_ref[...], preferred_element_type=F32)
        last = cs[:, tk - LANES:tk]
        total = jnp.broadcast_to(last[:, LANES - 1:LANES], (rows, LANES))
        is_total = lax.broadcasted_iota(jnp.int32, (rows, LANES), 1) == LANES - 1
        last = jnp.where(is_total, 0.0, last)
        if diag:
            cs_out[:, 0:tk - LANES] = cs[:, 0:tk - LANES]
            cs_out[:, tk - LANES:tk] = last
            tile_carry_ref[i] = total
        else:
            carry = jnp.where(c == i - 1, tile_carry_ref[i], carry_in[...])
            for lo in range(0, tk - LANES, LANES):
                cs_out[:, lo:lo + LANES] = cs[:, lo:lo + LANES] + carry
            cs_out[:, tk - LANES:tk] = last + carry
            carry_out[...] = carry + total

    def stage_c(zs_in, cs_in, a_out):
        a_out[...] = jnp.exp(zs_in[...] - cs_in[...]).astype(BF16)

    def stage_d(i, c, diag, a_in, acc_in, acc_out):
        a = jnp.concatenate([a_in[0:tq, :], a_in[tq:rows, :]], axis=1)
        acc = jnp.dot(a, vab_ref[c], preferred_element_type=F32)
        if diag:
            tile_acc_ref[i] = acc
        else:
            acc += jnp.where(c == i - 1, tile_acc_ref[i], acc_in[...])
            acc_out[...] = acc
        o_ref[block_rows(i, tq), :] = acc.astype(o_ref.dtype)

    def step(item_a, item_b1, item_b2, item_c, item_d, diag_b1, diag_b2, diag_d, slot):
        prev, prev2 = (slot - 1) % n_slots, (slot - 2) % n_slots
        if item_b2 is not None:
            stage_b2(*item_b2, diag_b2, h_ref.at[prev], cs_ref.at[slot],
                     carry_ref.at[prev], carry_ref.at[slot])
        if item_d is not None:
            stage_d(*item_d, diag_d, a_ref.at[prev], acc_ref.at[prev], acc_ref.at[slot])
        if item_a is not None:
            stage_a(*item_a, z_ref.at[slot])
        if item_c is not None:
            stage_c(zs_ref.at[prev2], cs_ref.at[prev], a_ref.at[slot])
        if item_b1 is not None:
            stage_b1(diag_b1, z_ref.at[prev], zs_ref.at[slot], h_ref.at[slot])

    def static_step(n):
        it = [items[n - d] if 0 <= n - d < len(items) else None for d in range(n_stages)]
        diag = [x is not None and x[0] == x[1] for x in it]
        step(*it, diag_b1=diag[1], diag_b2=diag[2], diag_d=diag[4], slot=n % n_slots)

    first = nq + n_stages - 1
    n_loops = (len(items) - first) // ATTN_UNROLL
    assert first % n_slots == 0
    for n in range(first):
        static_step(n)

    def advance(i, c):
        last = c == 0
        return jnp.where(last, i + 1, i), jnp.where(last, i, c - 1)

    def body(_, state):
        for u in range(ATTN_UNROLL):
            step(*state, False, False, False, u)
            state = (advance(*state[0]),) + state[:-1]
        return state

    init = tuple(tuple(jnp.int32(x) for x in items[first - d]) for d in range(n_stages))
    lax.fori_loop(0, n_loops, body, init)

    for n in range(first + ATTN_UNROLL * n_loops, len(items) + n_stages - 1):
        static_step(n)


def _tri_matrix():
    j = np.arange(ATTN_TK)[:, None]
    s = np.arange(ATTN_TK)[None, :]
    return jnp.asarray((j > s) | (s == ATTN_TK - 1), dtype=BF16)


def _causal_masks():
    t = (np.arange(2 * ATTN_TQ) % ATTN_TQ)[:, None]
    s = np.arange(ATTN_TK)[None, :]
    keep = (s < t).astype(np.float32)
    return jnp.asarray(keep), jnp.asarray((1.0 - keep) * np.float32(MASKED_LOGIT))


def _attention(q, k, v, batch, seq):
    tq, tk = ATTN_TQ, ATTN_TK
    assert tq == tk and seq % tq == 0
    n_pairs = D_ATTN // LANES
    head_pair = pl.BlockSpec((seq, LANES), lambda b, hp: (b, hp))
    const = lambda shape: pl.BlockSpec(shape, lambda b, hp: (0, 0))
    return pl.pallas_call(
        functools.partial(_attn_kernel, seq=seq),
        grid=(batch, n_pairs),
        in_specs=[head_pair, head_pair, head_pair, const((tk, tk)),
                  const((2 * tq, tk)), const((2 * tq, tk))],
        out_specs=head_pair,
        out_shape=jax.ShapeDtypeStruct(q.shape, BF16),
        scratch_shapes=[
            pltpu.VMEM((seq // tq, 2 * tq, LANES), BF16),
            pltpu.VMEM((seq // tk, 2 * tk, LANES), BF16),
            pltpu.VMEM((ATTN_UNROLL, 2 * tq, tk), F32),
            pltpu.VMEM((ATTN_UNROLL, 2 * tq, tk), F32),
            pltpu.VMEM((ATTN_UNROLL, 2 * tq, tk), BF16),
            pltpu.VMEM((ATTN_UNROLL, 2 * tq, tk), F32),
            pltpu.VMEM((ATTN_UNROLL, 2 * tq, tk), BF16),
            pltpu.VMEM((ATTN_UNROLL, 2 * tq, LANES), F32),
            pltpu.VMEM((ATTN_UNROLL, tq, LANES), F32),
            pltpu.VMEM((seq // tq, 2 * tq, LANES), F32),
            pltpu.VMEM((seq // tq, tq, LANES), F32),
        ],
        compiler_params=pltpu.CompilerParams(
            dimension_semantics=("arbitrary", "arbitrary"), vmem_limit_bytes=VMEM_LIMIT),
        name="stickbreak_attn",
    )(q, k, v, _tri_matrix(), *_causal_masks())


def _mix_kernel(u_ref, p_ref, ya_ref, x_ref, dww_ref, dwb_ref, lng_ref, lnb_ref, pww_ref,
                pwb_ref, poolw_ref, pools_ref, wout_ref, o_ref, ubuf, pbuf, ushift, psum, *, tm):
    s = pl.program_id(1)
    n_rows = HALO + tm

    @pl.when(s == 0)
    def _():
        ubuf[0:HALO, :] = jnp.zeros((HALO, D_CONV), F32)
        pbuf[0:HALO, :] = jnp.zeros((HALO, D_POOL), F32)

    ubuf[HALO:HALO + tm, :] = u_ref[...]
    pbuf[HALO:HALO + tm, :] = p_ref[...]

    n_stage = n_rows - SUBLANES
    for r in range(1, SUBLANES):
        ushift[r - 1, 0:n_stage, :] = ubuf[r:r + n_stage, :]
    conv = jnp.broadcast_to(dwb_ref[...], (tm, D_CONV))
    for k in range(CONV_K):
        start = HALO - (CONV_K - 1) + k
        r = start % SUBLANES
        base = start - r
        taps = ubuf[base:base + tm, :] if r == 0 else ushift[r - 1, base:base + tm, :]
        conv = conv + dww_ref[k:k + 1, :] * taps
    mu = jnp.mean(conv, axis=-1, keepdims=True)
    cen = conv - mu
    var = jnp.mean(cen * cen, axis=-1, keepdims=True)
    y = cen * lax.rsqrt(var + EPS) * lng_ref[...] + lnb_ref[...]
    y = y * jax.nn.sigmoid(y)
    y_conv = jnp.dot(y.astype(BF16), pww_ref[...], preferred_element_type=F32) + pwb_ref[...]

    lane = lax.broadcasted_iota(jnp.int32, (tm, D_POOL), 1)
    win = jnp.left_shift(2, lax.shift_right_logical(lane, int(np.log2(POOL_GROUP))))
    pos = s * tm + lax.broadcasted_iota(jnp.int32, (tm, D_POOL), 0)
    count = jnp.minimum(pos + 1, win).astype(F32)
    cur = pbuf[HALO:HALO + tm, :]
    assert POOL_WINDOWS == tuple(2 << l for l in range(N_POOL_GROUPS)) and HALO >= 8 * N_POOL_GROUPS
    src, wsum = pbuf, None
    for level in range(N_POOL_GROUPS):
        w = 1 << level
        lo = SUBLANES * (level + 1)
        if level < N_POOL_GROUPS - 1:
            psum[level, lo:n_rows, :] = src[lo:n_rows, :] + src[lo - w:n_rows - w, :]
            src = psum.at[level]
            s_w = src[HALO:n_rows, :]
        else:
            s_w = src[HALO:n_rows, :] + src[HALO - w:n_rows - w, :]
        wsum = s_w if wsum is None else jnp.where(win == 2 * w, s_w, wsum)
    pooled = wsum / count - cur
    y_pool = jnp.dot(pooled.astype(BF16), poolw_ref[...],
                     preferred_element_type=F32) * pools_ref[...]

    mix = jnp.dot(y_conv.astype(BF16), wout_ref[0:D_CONV, :], preferred_element_type=F32)
    mix += jnp.dot(y_pool.astype(BF16), wout_ref[D_CONV:D_CONV + D_POOL, :],
                   preferred_element_type=F32)
    mix += jnp.dot(ya_ref[...], wout_ref[D_CONV + D_POOL:, :], preferred_element_type=F32)
    o_ref[...] = x_ref[...] + mix

    ubuf[0:HALO, :] = ubuf[tm:tm + HALO, :]
    pbuf[0:HALO, :] = pbuf[tm:tm + HALO, :]


def _mixers(u, p, ya, x2, dww, dwb, lng, lnb, pww, pwb, poolw, pools, wout, batch, seq, tm):
    ns = seq // tm
    row = lambda b, s: (b * ns + s, 0)
    const = lambda b, s: (0, 0)
    full = lambda a: pl.BlockSpec(a.shape, const)
    return pl.pallas_call(
        functools.partial(_mix_kernel, tm=tm),
        grid=(batch, ns),
        in_specs=[
            pl.BlockSpec((tm, D_CONV), row),
            pl.BlockSpec((tm, D_POOL), row),
            pl.BlockSpec((tm, D_ATTN), row),
            pl.BlockSpec((tm, D_MODEL), row),
            full(dww), full(dwb), full(lng), full(lnb), full(pww), full(pwb),
            full(poolw), full(pools), full(wout),
        ],
        out_specs=pl.BlockSpec((tm, D_MODEL), row),
        out_shape=jax.ShapeDtypeStruct(x2.shape, F32),
        scratch_shapes=[
            pltpu.VMEM((HALO + tm, D_CONV), F32),
            pltpu.VMEM((HALO + tm, D_POOL), F32),
            pltpu.VMEM((SUBLANES - 1, HALO + tm, D_CONV), F32),
            pltpu.VMEM((N_POOL_GROUPS - 1, HALO + tm, D_POOL), F32),
        ],
        compiler_params=pltpu.CompilerParams(
            dimension_semantics=("arbitrary", "arbitrary"), vmem_limit_bytes=VMEM_LIMIT),
        name="mixers_outproj",
    )(u, p, ya, x2, dww, dwb, lng, lnb, pww, pwb, poolw, pools, wout)


def _shift_rows(g, prev, r):
    rolled = pltpu.roll(g, r, 0)
    prev_r = pltpu.roll(prev, r, 0)
    first = lax.broadcasted_iota(jnp.int32, prev.shape, 0) < r
    top = jnp.where(first, prev_r, rolled[0:SUBLANES, :])
    return jnp.concatenate([top, rolled[SUBLANES:, :]], axis=0)


def _ffn_kernel(x_ref, g2_ref, wv_ref, wg_ref, dww_ref, dwb_ref, wd_ref, fg_ref, o_ref,
                h_ref, prod_ref, carry_ref, *, tm, tf, final_norm):
    s = pl.program_id(1)
    x = x_ref[...]
    h_ref[...] = _rmsnorm(x, g2_ref[...]).astype(BF16)

    @pl.when(s == 0)
    def _():
        carry_ref[...] = jnp.zeros_like(carry_ref)

    for c in range(D_FF // tf):
        cols = slice(c * tf, (c + 1) * tf)
        h = h_ref[...]
        val = jnp.dot(h, wv_ref[:, cols], preferred_element_type=F32)
        g = jnp.dot(h, wg_ref[:, cols], preferred_element_type=F32)
        prev = carry_ref[:, cols]
        carry_ref[:, cols] = g[tm - SUBLANES:, :]
        conv = (dww_ref[0:1, cols] * _shift_rows(g, prev, 2)
                + dww_ref[1:2, cols] * _shift_rows(g, prev, 1)
                + dww_ref[2:3, cols] * g) + dwb_ref[:, cols]
        gate = 0.5 * conv * (1.0 + lax.erf(conv * np.float32(np.sqrt(0.5))))
        prod_ref[:, cols] = (val * gate).astype(BF16)

    y = x + jnp.dot(prod_ref[...], wd_ref[...], preferred_element_type=F32)
    if final_norm:
        y = _rmsnorm(y, fg_ref[...])
    o_ref[...] = y


def _ffn(x2, g2, w_up, dww, dwb, wd, fg, batch, seq, tm, tf, final_norm):
    ns = seq // tm
    row = lambda b, s: (b * ns + s, 0)
    const = lambda b, s: (0, 0)
    full = lambda a: pl.BlockSpec(a.shape, const)
    return pl.pallas_call(
        functools.partial(_ffn_kernel, tm=tm, tf=tf, final_norm=final_norm),
        grid=(batch, ns),
        in_specs=[pl.BlockSpec((tm, D_MODEL), row), full(g2),
                  pl.BlockSpec((D_MODEL, D_FF), lambda b, s: (0, 0)),
                  pl.BlockSpec((D_MODEL, D_FF), lambda b, s: (0, 1)),
                  full(dww), full(dwb), full(wd), full(fg)],
        out_specs=pl.BlockSpec((tm, D_MODEL), row),
        out_shape=jax.ShapeDtypeStruct(x2.shape, F32),
        scratch_shapes=[
            pltpu.VMEM((tm, D_MODEL), BF16),
            pltpu.VMEM((tm, D_FF), BF16),
            pltpu.VMEM((SUBLANES, D_FF), F32),
        ],
        compiler_params=pltpu.CompilerParams(
            dimension_semantics=("arbitrary", "arbitrary"), vmem_limit_bytes=VMEM_LIMIT),
        name="conv_ffn",
    )(x2, g2, w_up, w_up, dww, dwb, wd, fg)


def _block_diag(w):
    g, c, _ = w.shape
    eye = jnp.eye(g, dtype=w.dtype)
    return (eye[:, None, :, None] * w[:, :, None, :]).reshape(g * c, g * c)


def kernel(x, norm1_g, w_in, conv_dw_w, conv_dw_b, conv_ln_g, conv_ln_b, conv_pw_w, conv_pw_b,
           pool_w, pool_scale, w_out, norm2_g, ffn_up, ffn_dw_w, ffn_dw_b, ffn_down, final_g):
    batch, seq, d = x.shape
    depth = w_in.shape[0]
    x2 = x.reshape(batch * seq, d)
    vec = lambda a: a.reshape(1, -1)
    for l in range(depth):
        u, p, q, k, v = _inproj(x2, vec(norm1_g[l]), w_in[l].astype(BF16), tm=512)
        ya = _attention(q, k, v, batch, seq)
        x2 = _mixers(u, p, ya, x2, conv_dw_w[l], vec(conv_dw_b[l]), vec(conv_ln_g[l]),
                     vec(conv_ln_b[l]), conv_pw_w[l].astype(BF16), vec(conv_pw_b[l]),
                     _block_diag(pool_w[l]).astype(BF16), vec(pool_scale[l]),
                     w_out[l].astype(BF16), batch, seq, tm=512)
        x2 = _ffn(x2, vec(norm2_g[l]), ffn_up[l].astype(BF16), ffn_dw_w[l],
                  vec(ffn_dw_b[l]), ffn_down[l].astype(BF16), vec(final_g), batch, seq,
                  tm=512, tf=256, final_norm=(l == depth - 1))
    return x2.reshape(batch, seq, d)
```

```python
import functools

import jax
import jax.numpy as jnp
import numpy as np
from jax import lax
from jax.experimental import pallas as pl
from jax.experimental.pallas import tpu as pltpu

F32 = jnp.float32
BF16 = jnp.bfloat16

D_MODEL = 1024
D_CONV = 256
D_POOL = 256
D_ATTN = 512
N_HEADS = 8
HEAD_DIM = 64
CONV_K = 31
POOL_WINDOWS = (2, 4, 8, 16)
N_POOL_GROUPS = len(POOL_WINDOWS)
POOL_GROUP = 64
D_FF = 2816
FFN_CONV_K = 3
EPS = 1e-6

LANES = 128
SUBLANES = 8
HALO = 32
ATTN_TQ = 256
ATTN_TK = 256
ATTN_UNROLL = 12
MASKED_LOGIT = -1e30
LOG2E = float(np.log2(np.e))
VMEM_LIMIT = 56 * 1024 * 1024


def _rmsnorm(x, g):
    ms = jnp.mean(x * x, axis=-1, keepdims=True)
    return (x * lax.rsqrt(ms + EPS)) * g


def _inproj_kernel(x_ref, g_ref, w_ref, u_ref, p_ref, q_ref, k_ref, v_ref):
    h = _rmsnorm(x_ref[...], g_ref[...]).astype(BF16)

    def proj(lo, hi):
        return jnp.dot(h, w_ref[:, lo:hi], preferred_element_type=F32)

    o = 0
    a = proj(o, o + D_CONV); o += D_CONV
    gate = proj(o, o + D_CONV); o += D_CONV
    u_ref[...] = a * jax.nn.sigmoid(gate)
    p_ref[...] = proj(o, o + D_POOL); o += D_POOL
    q_ref[...] = (proj(o, o + D_ATTN) * (HEAD_DIM ** -0.5)).astype(BF16); o += D_ATTN
    k_ref[...] = proj(o, o + D_ATTN).astype(BF16); o += D_ATTN
    v_ref[...] = proj(o, o + D_ATTN).astype(BF16)


def _layer(w, layer, index_map=lambda *_: (0, 0)):
    return pl.BlockSpec((None,) + w.shape[1:], lambda *idx: (layer,) + index_map(*idx))


def _inproj(x2, g, w, layer, tm):
    t = x2.shape[0]
    d_in = w.shape[2]
    row = lambda i: (i, 0)
    const = lambda i: (0, 0)
    return pl.pallas_call(
        _inproj_kernel,
        grid=(t // tm,),
        in_specs=[
            pl.BlockSpec((tm, D_MODEL), row),
            pl.BlockSpec((1, D_MODEL), const),
            _layer(w, layer),
        ],
        out_specs=[
            pl.BlockSpec((tm, D_CONV), row),
            pl.BlockSpec((tm, D_POOL), row),
            pl.BlockSpec((tm, D_ATTN), row),
            pl.BlockSpec((tm, D_ATTN), row),
            pl.BlockSpec((tm, D_ATTN), row),
        ],
        out_shape=[
            jax.ShapeDtypeStruct((t, D_CONV), F32),
            jax.ShapeDtypeStruct((t, D_POOL), F32),
            jax.ShapeDtypeStruct((t, D_ATTN), BF16),
            jax.ShapeDtypeStruct((t, D_ATTN), BF16),
            jax.ShapeDtypeStruct((t, D_ATTN), BF16),
        ],
        compiler_params=pltpu.CompilerParams(
            dimension_semantics=("arbitrary",), vmem_limit_bytes=VMEM_LIMIT),
        name="inproj",
    )(x2, g, w)


def _attn_kernel(q_ref, k_ref, v_ref, tri_ref, keep_ref, off_ref, o_ref, qm_ref, vab_ref, z_ref,
                 zs_ref, h_ref, cs_ref, a_ref, carry_ref, acc_ref, tile_carry_ref, tile_acc_ref,
                 *, seq):
    tq, tk = ATTN_TQ, ATTN_TK
    rows = 2 * tq
    nq = seq // tq
    off_diag = [(i, c) for i in range(1, nq) for c in range(i - 1, -1, -1)]
    items = [(i, i) for i in range(nq)] + off_diag
    n_stages = 5
    n_slots = ATTN_UNROLL

    head_a = lax.broadcasted_iota(jnp.int32, (tq, LANES), 1) < HEAD_DIM
    for i in range(nq):
        q = q_ref[i * tq:(i + 1) * tq, :]
        v = v_ref[i * tq:(i + 1) * tq, :]
        zero = jnp.zeros_like(q)
        qm_ref[i, 0:tq, :] = jnp.where(head_a, q, zero)
        qm_ref[i, tq:rows, :] = jnp.where(head_a, zero, q)
        vab_ref[i, 0:tk, :] = jnp.where(head_a, v, zero)
        vab_ref[i, tk:2 * tk, :] = jnp.where(head_a, zero, v)

    def block_rows(idx, size):
        return pl.ds(pl.multiple_of(idx * size, size), size)

    def stage_a(i, c, z_out):
        z_out[...] = lax.dot_general(qm_ref[i], k_ref[block_rows(c, tk), :],
                                     (((1,), (1,)), ((), ())), preferred_element_type=F32)

    def stage_b1(diag, z_in, zs_out, h_out):
        z = z_in[...]
        sp = jnp.maximum(z, 0.0) + jnp.log(1.0 + jnp.exp2(jnp.abs(z) * -LOG2E))
        zs = z - sp
        if diag:
            sp = sp * keep_ref[...]
            zs = zs * keep_ref[...] + off_ref[...]
        zs_out[...] = zs
        h_out[...] = sp.astype(BF16)

    def stage_b2(i, c, diag, h_in, cs_out, carry_in, carry_out):
        cs = jnp.dot(h_in[...], tri_ref[...], preferred_element_type=F32)
        last = cs[:, tk - LANES:tk]
        total = jnp.broadcast_to(last[:, LANES - 1:LANES], (rows, LANES))
        is_total = lax.broadcasted_iota(jnp.int32, (rows, LANES), 1) == LANES - 1
        last = jnp.where(is_total, 0.0, last)
        if diag:
            cs_out[:, 0:tk - LANES] = cs[:, 0:tk - LANES]
            cs_out[:, tk - LANES:tk] = last
            tile_carry_ref[i] = total
        else:
            carry = jnp.where(c == i - 1, tile_carry_ref[i], carry_in[...])
            for lo in range(0, tk - LANES, LANES):
                cs_out[:, lo:lo + LANES] = cs[:, lo:lo + LANES] + carry
            cs_out[:, tk - LANES:tk] = last + carry
            carry_out[...] = carry + total

    def stage_c(zs_in, cs_in, a_out):
        a_out[...] = jnp.exp(zs_in[...] - cs_in[...]).astype(BF16)

    def stage_d(i, c, diag, a_in, acc_in, acc_out):
        a = jnp.concatenate([a_in[0:tq, :], a_in[tq:rows, :]], axis=1)
        acc = jnp.dot(a, vab_ref[c], preferred_element_type=F32)
        if diag:
            tile_acc_ref[i] = acc
        else:
            acc += jnp.where(c == i - 1, tile_acc_ref[i], acc_in[...])
            acc_out[...] = acc
        o_ref[block_rows(i, tq), :] = acc.astype(o_ref.dtype)

    def step(item_a, item_b1, item_b2, item_c, item_d, diag_b1, diag_b2, diag_d, slot):
        prev, prev2 = (slot - 1) % n_slots, (slot - 2) % n_slots
        if item_b2 is not None:
            stage_b2(*item_b2, diag_b2, h_ref.at[prev], cs_ref.at[slot],
                     carry_ref.at[prev], carry_ref.at[slot])
        if item_d is not None:
            stage_d(*item_d, diag_d, a_ref.at[prev], acc_ref.at[prev], acc_ref.at[slot])
        if item_a is not None:
            stage_a(*item_a, z_ref.at[slot])
        if item_c is not None:
            stage_c(zs_ref.at[prev2], cs_ref.at[prev], a_ref.at[slot])
        if item_b1 is not None:
            stage_b1(diag_b1, z_ref.at[prev], zs_ref.at[slot], h_ref.at[slot])

    def static_step(n):
        it = [items[n - d] if 0 <= n - d < len(items) else None for d in range(n_stages)]
        diag = [x is not None and x[0] == x[1] for x in it]
        step(*it, diag_b1=diag[1], diag_b2=diag[2], diag_d=diag[4], slot=n % n_slots)

    first = nq + n_stages - 1
    n_loops = (len(items) - first) // ATTN_UNROLL
    assert first % n_slots == 0
    for n in range(first):
        static_step(n)

    def advance(i, c):
        last = c == 0
        return jnp.where(last, i + 1, i), jnp.where(last, i, c - 1)

    def body(_, state):
        for u in range(ATTN_UNROLL):
            step(*state, False, False, False, u)
            state = (advance(*state[0]),) + state[:-1]
        return state

    init = tuple(tuple(jnp.int32(x) for x in items[first - d]) for d in range(n_stages))
    lax.fori_loop(0, n_loops, body, init)

    for n in range(first + ATTN_UNROLL * n_loops, len(items) + n_stages - 1):
        static_step(n)


def _tri_matrix():
    j = np.arange(ATTN_TK)[:, None]
    s = np.arange(ATTN_TK)[None, :]
    return jnp.asarray((j > s) | (s == ATTN_TK - 1), dtype=BF16)


def _causal_masks():
    t = (np.arange(2 * ATTN_TQ) % ATTN_TQ)[:, None]
    s = np.arange(ATTN_TK)[None, :]
    keep = (s < t).astype(np.float32)
    return jnp.asarray(keep), jnp.asarray((1.0 - keep) * np.float32(MASKED_LOGIT))


def _attention(q, k, v, batch, seq):
    tq, tk = ATTN_TQ, ATTN_TK
    assert tq == tk and seq % tq == 0
    n_pairs = D_ATTN // LANES
    head_pair = pl.BlockSpec((seq, LANES), lambda b, hp: (b, hp))
    const = lambda shape: pl.BlockSpec(shape, lambda b, hp: (0, 0))
    return pl.pallas_call(
        functools.partial(_attn_kernel, seq=seq),
        grid=(batch, n_pairs),
        in_specs=[head_pair, head_pair, head_pair, const((tk, tk)),
                  const((2 * tq, tk)), const((2 * tq, tk))],
        out_specs=head_pair,
        out_shape=jax.ShapeDtypeStruct(q.shape, BF16),
        scratch_shapes=[
            pltpu.VMEM((seq // tq, 2 * tq, LANES), BF16),
            pltpu.VMEM((seq // tk, 2 * tk, LANES), BF16),
            pltpu.VMEM((ATTN_UNROLL, 2 * tq, tk), F32),
            pltpu.VMEM((ATTN_UNROLL, 2 * tq, tk), F32),
            pltpu.VMEM((ATTN_UNROLL, 2 * tq, tk), BF16),
            pltpu.VMEM((ATTN_UNROLL, 2 * tq, tk), F32),
            pltpu.VMEM((ATTN_UNROLL, 2 * tq, tk), BF16),
            pltpu.VMEM((ATTN_UNROLL, 2 * tq, LANES), F32),
            pltpu.VMEM((ATTN_UNROLL, tq, LANES), F32),
            pltpu.VMEM((seq // tq, 2 * tq, LANES), F32),
            pltpu.VMEM((seq // tq, tq, LANES), F32),
        ],
        compiler_params=pltpu.CompilerParams(
            dimension_semantics=("arbitrary", "arbitrary"), vmem_limit_bytes=VMEM_LIMIT),
        name="stickbreak_attn",
    )(q, k, v, _tri_matrix(), *_causal_masks())


def _mix_kernel(u_ref, p_ref, ya_ref, x_ref, dww_ref, dwb_ref, lng_ref, lnb_ref, pww_ref,
                pwb_ref, poolw_ref, pools_ref, wout_ref, o_ref, ubuf, pbuf, ushift, psum, *, tm):
    s = pl.program_id(1)
    n_rows = HALO + tm

    @pl.when(s == 0)
    def _():
        ubuf[0:HALO, :] = jnp.zeros((HALO, D_CONV), F32)
        pbuf[0:HALO, :] = jnp.zeros((HALO, D_POOL), F32)

    ubuf[HALO:HALO + tm, :] = u_ref[...]
    pbuf[HALO:HALO + tm, :] = p_ref[...]

    n_stage = n_rows - SUBLANES
    for r in range(1, SUBLANES):
        ushift[r - 1, 0:n_stage, :] = ubuf[r:r + n_stage, :]
    conv = jnp.broadcast_to(dwb_ref[...], (tm, D_CONV))
    for k in range(CONV_K):
        start = HALO - (CONV_K - 1) + k
        r = start % SUBLANES
        base = start - r
        taps = ubuf[base:base + tm, :] if r == 0 else ushift[r - 1, base:base + tm, :]
        conv = conv + dww_ref[k:k + 1, :] * taps
    mu = jnp.mean(conv, axis=-1, keepdims=True)
    cen = conv - mu
    var = jnp.mean(cen * cen, axis=-1, keepdims=True)
    y = cen * lax.rsqrt(var + EPS) * lng_ref[...] + lnb_ref[...]
    y = y * jax.nn.sigmoid(y)
    y_conv = jnp.dot(y.astype(BF16), pww_ref[...], preferred_element_type=F32) + pwb_ref[...]

    lane = lax.broadcasted_iota(jnp.int32, (tm, D_POOL), 1)
    win = jnp.left_shift(2, lax.shift_right_logical(lane, int(np.log2(POOL_GROUP))))
    pos = s * tm + lax.broadcasted_iota(jnp.int32, (tm, D_POOL), 0)
    count = jnp.minimum(pos + 1, win).astype(F32)
    cur = pbuf[HALO:HALO + tm, :]
    assert POOL_WINDOWS == tuple(2 << l for l in range(N_POOL_GROUPS)) and HALO >= 8 * N_POOL_GROUPS
    src, wsum = pbuf, None
    for level in range(N_POOL_GROUPS):
        w = 1 << level
        lo = SUBLANES * (level + 1)
        if level < N_POOL_GROUPS - 1:
            psum[level, lo:n_rows, :] = src[lo:n_rows, :] + src[lo - w:n_rows - w, :]
            src = psum.at[level]
            s_w = src[HALO:n_rows, :]
        else:
            s_w = src[HALO:n_rows, :] + src[HALO - w:n_rows - w, :]
        wsum = s_w if wsum is None else jnp.where(win == 2 * w, s_w, wsum)
    pooled = wsum / count - cur
    y_pool = jnp.dot(pooled.astype(BF16), poolw_ref[...],
                     preferred_element_type=F32) * pools_ref[...]

    mix = jnp.dot(y_conv.astype(BF16), wout_ref[0:D_CONV, :], preferred_element_type=F32)
    mix += jnp.dot(y_pool.astype(BF16), wout_ref[D_CONV:D_CONV + D_POOL, :],
                   preferred_element_type=F32)
    mix += jnp.dot(ya_ref[...], wout_ref[D_CONV + D_POOL:, :], preferred_element_type=F32)
    o_ref[...] = x_ref[...] + mix

    ubuf[0:HALO, :] = ubuf[tm:tm + HALO, :]
    pbuf[0:HALO, :] = pbuf[tm:tm + HALO, :]


def _mixers(u, p, ya, x2, dww, dwb, lng, lnb, pww, pwb, poolw, pools, wout, layer, batch, seq,
            tm):
    ns = seq // tm
    row = lambda b, s: (b * ns + s, 0)
    const = lambda b, s: (0, 0)
    full = lambda a: pl.BlockSpec(a.shape, const)
    return pl.pallas_call(
        functools.partial(_mix_kernel, tm=tm),
        grid=(batch, ns),
        in_specs=[
            pl.BlockSpec((tm, D_CONV), row),
            pl.BlockSpec((tm, D_POOL), row),
            pl.BlockSpec((tm, D_ATTN), row),
            pl.BlockSpec((tm, D_MODEL), row),
            full(dww), full(dwb), full(lng), full(lnb), _layer(pww, layer), full(pwb),
            _layer(poolw, layer), full(pools), _layer(wout, layer),
        ],
        out_specs=pl.BlockSpec((tm, D_MODEL), row),
        out_shape=jax.ShapeDtypeStruct(x2.shape, F32),
        scratch_shapes=[
            pltpu.VMEM((HALO + tm, D_CONV), F32),
            pltpu.VMEM((HALO + tm, D_POOL), F32),
            pltpu.VMEM((SUBLANES - 1, HALO + tm, D_CONV), F32),
            pltpu.VMEM((N_POOL_GROUPS - 1, HALO + tm, D_POOL), F32),
        ],
        compiler_params=pltpu.CompilerParams(
            dimension_semantics=("arbitrary", "arbitrary"), vmem_limit_bytes=VMEM_LIMIT),
        name="mixers_outproj",
    )(u, p, ya, x2, dww, dwb, lng, lnb, pww, pwb, poolw, pools, wout)


def _shift_rows(g, prev, r):
    rolled = pltpu.roll(g, r, 0)
    prev_r = pltpu.roll(prev, r, 0)
    first = lax.broadcasted_iota(jnp.int32, prev.shape, 0) < r
    top = jnp.where(first, prev_r, rolled[0:SUBLANES, :])
    return jnp.concatenate([top, rolled[SUBLANES:, :]], axis=0)


def _ffn_kernel(x_ref, g2_ref, wv_ref, wg_ref, dww_ref, dwb_ref, wd_ref, fg_ref, o_ref,
                h_ref, prod_ref, carry_ref, *, tm, tf, final_norm):
    s = pl.program_id(1)
    x = x_ref[...]
    h_ref[...] = _rmsnorm(x, g2_ref[...]).astype(BF16)

    @pl.when(s == 0)
    def _():
        carry_ref[...] = jnp.zeros_like(carry_ref)

    for c in range(D_FF // tf):
        cols = slice(c * tf, (c + 1) * tf)
        h = h_ref[...]
        val = jnp.dot(h, wv_ref[:, cols], preferred_element_type=F32)
        g = jnp.dot(h, wg_ref[:, cols], preferred_element_type=F32)
        prev = carry_ref[:, cols]
        carry_ref[:, cols] = g[tm - SUBLANES:, :]
        conv = (dww_ref[0:1, cols] * _shift_rows(g, prev, 2)
                + dww_ref[1:2, cols] * _shift_rows(g, prev, 1)
                + dww_ref[2:3, cols] * g) + dwb_ref[:, cols]
        gate = 0.5 * conv * (1.0 + lax.erf(conv * np.float32(np.sqrt(0.5))))
        prod_ref[:, cols] = (val * gate).astype(BF16)

    y = x + jnp.dot(prod_ref[...], wd_ref[...], preferred_element_type=F32)
    if final_norm:
        y = _rmsnorm(y, fg_ref[...])
    o_ref[...] = y


def _ffn(x2, g2, w_up, dww, dwb, wd, fg, layer, batch, seq, tm, tf, final_norm):
    ns = seq // tm
    row = lambda b, s: (b * ns + s, 0)
    const = lambda b, s: (0, 0)
    full = lambda a: pl.BlockSpec(a.shape, const)
    return pl.pallas_call(
        functools.partial(_ffn_kernel, tm=tm, tf=tf, final_norm=final_norm),
        grid=(batch, ns),
        in_specs=[pl.BlockSpec((tm, D_MODEL), row), full(g2),
                  pl.BlockSpec((None, D_MODEL, D_FF), lambda b, s: (layer, 0, 0)),
                  pl.BlockSpec((None, D_MODEL, D_FF), lambda b, s: (layer, 0, 1)),
                  full(dww), full(dwb), _layer(wd, layer), full(fg)],
        out_specs=pl.BlockSpec((tm, D_MODEL), row),
        out_shape=jax.ShapeDtypeStruct(x2.shape, F32),
        scratch_shapes=[
            pltpu.VMEM((tm, D_MODEL), BF16),
            pltpu.VMEM((tm, D_FF), BF16),
            pltpu.VMEM((SUBLANES, D_FF), F32),
        ],
        compiler_params=pltpu.CompilerParams(
            dimension_semantics=("arbitrary", "arbitrary"), vmem_limit_bytes=VMEM_LIMIT),
        name="conv_ffn",
    )(x2, g2, w_up, w_up, dww, dwb, wd, fg)


def _block_diag(w):
    n, g, c, _ = w.shape
    eye = jnp.eye(g, dtype=w.dtype)
    return (eye[None, :, None, :, None] * w[:, :, :, None, :]).reshape(n, g * c, g * c)


def kernel(x, norm1_g, w_in, conv_dw_w, conv_dw_b, conv_ln_g, conv_ln_b, conv_pw_w, conv_pw_b,
           pool_w, pool_scale, w_out, norm2_g, ffn_up, ffn_dw_w, ffn_dw_b, ffn_down, final_g):
    batch, seq, d = x.shape
    depth = w_in.shape[0]
    x2 = x.reshape(batch * seq, d)
    vec = lambda a: a.reshape(1, -1)
    w_in_b, pw_b, w_out_b = w_in.astype(BF16), conv_pw_w.astype(BF16), w_out.astype(BF16)
    pool_b, up_b, down_b = _block_diag(pool_w).astype(BF16), ffn_up.astype(BF16), ffn_down.astype(BF16)
    for l in range(depth):
        u, p, q, k, v = _inproj(x2, vec(norm1_g[l]), w_in_b, l, tm=1024)
        ya = _attention(q, k, v, batch, seq)
        x2 = _mixers(u, p, ya, x2, conv_dw_w[l], vec(conv_dw_b[l]), vec(conv_ln_g[l]),
                     vec(conv_ln_b[l]), pw_b, vec(conv_pw_b[l]), pool_b, vec(pool_scale[l]),
                     w_out_b, l, batch, seq, tm=1024)
        x2 = _ffn(x2, vec(norm2_g[l]), up_b, ffn_dw_w[l], vec(ffn_dw_b[l]), down_b,
                  vec(final_g), l, batch, seq, tm=512, tf=256, final_norm=(l == depth - 1))
    return x2.reshape(batch, seq, d)
```

```python
import functools

import jax
import jax.numpy as jnp
import numpy as np
from jax import lax
from jax.experimental import pallas as pl
from jax.experimental.pallas import tpu as pltpu

F32 = jnp.float32
BF16 = jnp.bfloat16

D_MODEL = 1024
D_CONV = 256
D_POOL = 256
D_ATTN = 512
N_HEADS = 8
HEAD_DIM = 64
CONV_K = 31
POOL_WINDOWS = (2, 4, 8, 16)
N_POOL_GROUPS = len(POOL_WINDOWS)
POOL_GROUP = 64
D_FF = 2816
FFN_CONV_K = 3
EPS = 1e-6

LANES = 128
SUBLANES = 8
HALO = 32
ATTN_TQ = 256
ATTN_TK = 256
ATTN_UNROLL = 12
MASKED_LOGIT = -1e30
LOG2E = float(np.log2(np.e))
VMEM_LIMIT = 56 * 1024 * 1024


def _rmsnorm(x, g):
    ms = jnp.mean(x * x, axis=-1, keepdims=True)
    return (x * lax.rsqrt(ms + EPS)) * g


def _inproj_kernel(x_ref, g_ref, w_ref, u_ref, p_ref, q_ref, k_ref, v_ref):
    h = _rmsnorm(x_ref[...], g_ref[...]).astype(BF16)

    def proj(lo, hi):
        return jnp.dot(h, w_ref[:, lo:hi], preferred_element_type=F32)

    o = 0
    a = proj(o, o + D_CONV); o += D_CONV
    gate = proj(o, o + D_CONV); o += D_CONV
    u_ref[...] = a * jax.nn.sigmoid(gate)
    p_ref[...] = proj(o, o + D_POOL); o += D_POOL
    q_ref[...] = (proj(o, o + D_ATTN) * (HEAD_DIM ** -0.5)).astype(BF16); o += D_ATTN
    k_ref[...] = proj(o, o + D_ATTN).astype(BF16); o += D_ATTN
    v_ref[...] = proj(o, o + D_ATTN).astype(BF16)


def _layer(w, layer, index_map=lambda *_: (0, 0)):
    return pl.BlockSpec((None,) + w.shape[1:], lambda *idx: (layer,) + index_map(*idx))


def _inproj(x2, g, w, layer, tm):
    t = x2.shape[0]
    d_in = w.shape[2]
    row = lambda i: (i, 0)
    const = lambda i: (0, 0)
    return pl.pallas_call(
        _inproj_kernel,
        grid=(t // tm,),
        in_specs=[
            pl.BlockSpec((tm, D_MODEL), row),
            pl.BlockSpec((1, D_MODEL), const),
            _layer(w, layer),
        ],
        out_specs=[
            pl.BlockSpec((tm, D_CONV), row),
            pl.BlockSpec((tm, D_POOL), row),
            pl.BlockSpec((tm, D_ATTN), row),
            pl.BlockSpec((tm, D_ATTN), row),
            pl.BlockSpec((tm, D_ATTN), row),
        ],
        out_shape=[
            jax.ShapeDtypeStruct((t, D_CONV), F32),
            jax.ShapeDtypeStruct((t, D_POOL), F32),
            jax.ShapeDtypeStruct((t, D_ATTN), BF16),
            jax.ShapeDtypeStruct((t, D_ATTN), BF16),
            jax.ShapeDtypeStruct((t, D_ATTN), BF16),
        ],
        compiler_params=pltpu.CompilerParams(
            dimension_semantics=("arbitrary",), vmem_limit_bytes=VMEM_LIMIT),
        name="inproj",
    )(x2, g, w)


def _attn_kernel(q_ref, k_ref, v_ref, tri_ref, keep_ref, off_ref, o_ref, qm_ref, vab_ref, z_ref,
                 zs_ref, h_ref, cs_ref, a_ref, carry_ref, acc_ref, tile_carry_ref, tile_acc_ref,
                 *, seq):
    tq, tk = ATTN_TQ, ATTN_TK
    rows = 2 * tq
    nq = seq // tq
    off_diag = [(i, c) for i in range(1, nq) for c in range(i - 1, -1, -1)]
    items = [(i, i) for i in range(nq)] + off_diag
    n_stages = 5
    n_slots = ATTN_UNROLL

    head_a = lax.broadcasted_iota(jnp.int32, (tq, LANES), 1) < HEAD_DIM
    for i in range(nq):
        q = q_ref[i * tq:(i + 1) * tq, :]
        v = v_ref[i * tq:(i + 1) * tq, :]
        zero = jnp.zeros_like(q)
        qm_ref[i, 0:tq, :] = jnp.where(head_a, q, zero)
        qm_ref[i, tq:rows, :] = jnp.where(head_a, zero, q)
        vab_ref[i, 0:tk, :] = jnp.where(head_a, v, zero)
        vab_ref[i, tk:2 * tk, :] = jnp.where(head_a, zero, v)

    def block_rows(idx, size):
        return pl.ds(pl.multiple_of(idx * size, size), size)

    def stage_a(i, c, z_out):
        z_out[...] = lax.dot_general(qm_ref[i], k_ref[block_rows(c, tk), :],
                                     (((1,), (1,)), ((), ())), preferred_element_type=F32)

    def stage_b1(diag, z_in, zs_out, h_out):
        z = z_in[...]
        sp = jnp.maximum(z, 0.0) + jnp.log(1.0 + jnp.exp2(jnp.abs(z) * -LOG2E))
        zs = z - sp
        if diag:
            sp = sp * keep_ref[...]
            zs = zs * keep_ref[...] + off_ref[...]
        zs_out[...] = zs
        h_out[...] = sp.astype(BF16)

    def stage_b2(i, c, diag, h_in, cs_out, carry_in, carry_out):
        cs = jnp.dot(h_in[...], tri_ref[...], preferred_element_type=F32)
        last = cs[:, tk - LANES:tk]
        total = jnp.broadcast_to(last[:, LANES - 1:LANES], (rows, LANES))
        is_total = lax.broadcasted_iota(jnp.int32, (rows, LANES), 1) == LANES - 1
        last = jnp.where(is_total, 0.0, last)
        if diag:
            cs_out[:, 0:tk - LANES] = cs[:, 0:tk - LANES]
            cs_out[:, tk - LANES:tk] = last
            tile_carry_ref[i] = total
        else:
            carry = jnp.where(c == i - 1, tile_carry_ref[i], carry_in[...])
            for lo in range(0, tk - LANES, LANES):
                cs_out[:, lo:lo + LANES] = cs[:, lo:lo + LANES] + carry
            cs_out[:, tk - LANES:tk] = last + carry
            carry_out[...] = carry + total

    def stage_c(zs_in, cs_in, a_out):
        a_out[...] = jnp.exp((zs_in[...] - cs_in[...]).astype(BF16))

    def stage_d(i, c, diag, a_in, acc_in, acc_out):
        a = jnp.concatenate([a_in[0:tq, :], a_in[tq:rows, :]], axis=1)
        acc = jnp.dot(a, vab_ref[c], preferred_element_type=F32)
        if diag:
            tile_acc_ref[i] = acc
        else:
            acc += jnp.where(c == i - 1, tile_acc_ref[i], acc_in[...])
            acc_out[...] = acc
        o_ref[block_rows(i, tq), :] = acc.astype(o_ref.dtype)

    def step(item_a, item_b1, item_b2, item_c, item_d, diag_b1, diag_b2, diag_d, slot):
        prev, prev2 = (slot - 1) % n_slots, (slot - 2) % n_slots
        if item_b2 is not None:
            stage_b2(*item_b2, diag_b2, h_ref.at[prev], cs_ref.at[slot],
                     carry_ref.at[prev], carry_ref.at[slot])
        if item_d is not None:
            stage_d(*item_d, diag_d, a_ref.at[prev], acc_ref.at[prev], acc_ref.at[slot])
        if item_a is not None:
            stage_a(*item_a, z_ref.at[slot])
        if item_c is not None:
            stage_c(zs_ref.at[prev2], cs_ref.at[prev], a_ref.at[slot])
        if item_b1 is not None:
            stage_b1(diag_b1, z_ref.at[prev], zs_ref.at[slot], h_ref.at[slot])

    def static_step(n):
        it = [items[n - d] if 0 <= n - d < len(items) else None for d in range(n_stages)]
        diag = [x is not None and x[0] == x[1] for x in it]
        step(*it, diag_b1=diag[1], diag_b2=diag[2], diag_d=diag[4], slot=n % n_slots)

    first = nq + n_stages - 1
    n_loops = (len(items) - first) // ATTN_UNROLL
    assert first % n_slots == 0
    for n in range(first):
        static_step(n)

    def advance(i, c):
        last = c == 0
        return jnp.where(last, i + 1, i), jnp.where(last, i, c - 1)

    def body(_, state):
        for u in range(ATTN_UNROLL):
            step(*state, False, False, False, u)
            state = (advance(*state[0]),) + state[:-1]
        return state

    init = tuple(tuple(jnp.int32(x) for x in items[first - d]) for d in range(n_stages))
    lax.fori_loop(0, n_loops, body, init)

    for n in range(first + ATTN_UNROLL * n_loops, len(items) + n_stages - 1):
        static_step(n)


def _tri_matrix():
    j = np.arange(ATTN_TK)[:, None]
    s = np.arange(ATTN_TK)[None, :]
    return jnp.asarray((j > s) | (s == ATTN_TK - 1), dtype=BF16)


def _causal_masks():
    t = (np.arange(2 * ATTN_TQ) % ATTN_TQ)[:, None]
    s = np.arange(ATTN_TK)[None, :]
    keep = (s < t).astype(np.float32)
    return jnp.asarray(keep), jnp.asarray((1.0 - keep) * np.float32(MASKED_LOGIT))


def _attention(q, k, v, batch, seq):
    tq, tk = ATTN_TQ, ATTN_TK
    assert tq == tk and seq % tq == 0
    n_pairs = D_ATTN // LANES
    head_pair = pl.BlockSpec((seq, LANES), lambda b, hp: (b, hp))
    const = lambda shape: pl.BlockSpec(shape, lambda b, hp: (0, 0))
    return pl.pallas_call(
        functools.partial(_attn_kernel, seq=seq),
        grid=(batch, n_pairs),
        in_specs=[head_pair, head_pair, head_pair, const((tk, tk)),
                  const((2 * tq, tk)), const((2 * tq, tk))],
        out_specs=head_pair,
        out_shape=jax.ShapeDtypeStruct(q.shape, BF16),
        scratch_shapes=[
            pltpu.VMEM((seq // tq, 2 * tq, LANES), BF16),
            pltpu.VMEM((seq // tk, 2 * tk, LANES), BF16),
            pltpu.VMEM((ATTN_UNROLL, 2 * tq, tk), F32),
            pltpu.VMEM((ATTN_UNROLL, 2 * tq, tk), F32),
            pltpu.VMEM((ATTN_UNROLL, 2 * tq, tk), BF16),
            pltpu.VMEM((ATTN_UNROLL, 2 * tq, tk), F32),
            pltpu.VMEM((ATTN_UNROLL, 2 * tq, tk), BF16),
            pltpu.VMEM((ATTN_UNROLL, 2 * tq, LANES), F32),
            pltpu.VMEM((ATTN_UNROLL, tq, LANES), F32),
            pltpu.VMEM((seq // tq, 2 * tq, LANES), F32),
            pltpu.VMEM((seq // tq, tq, LANES), F32),
        ],
        compiler_params=pltpu.CompilerParams(
            dimension_semantics=("arbitrary", "arbitrary"), vmem_limit_bytes=VMEM_LIMIT),
        name="stickbreak_attn",
    )(q, k, v, _tri_matrix(), *_causal_masks())


def _mix_kernel(u_ref, p_ref, ya_ref, x_ref, dww_ref, dwb_ref, lng_ref, lnb_ref, pww_ref,
                pwb_ref, poolw_ref, pools_ref, wout_ref, o_ref, ubuf, pbuf, ushift, psum, *, tm):
    s = pl.program_id(1)
    n_rows = HALO + tm

    @pl.when(s == 0)
    def _():
        ubuf[0:HALO, :] = jnp.zeros((HALO, D_CONV), F32)
        pbuf[0:HALO, :] = jnp.zeros((HALO, D_POOL), F32)

    ubuf[HALO:HALO + tm, :] = u_ref[...]
    pbuf[HALO:HALO + tm, :] = p_ref[...]

    n_stage = n_rows - SUBLANES
    for r in range(1, SUBLANES):
        ushift[r - 1, 0:n_stage, :] = ubuf[r:r + n_stage, :]
    conv = jnp.broadcast_to(dwb_ref[...], (tm, D_CONV))
    for k in range(CONV_K):
        start = HALO - (CONV_K - 1) + k
        r = start % SUBLANES
        base = start - r
        taps = ubuf[base:base + tm, :] if r == 0 else ushift[r - 1, base:base + tm, :]
        conv = conv + dww_ref[k:k + 1, :] * taps
    mu = jnp.mean(conv, axis=-1, keepdims=True)
    cen = conv - mu
    var = jnp.mean(cen * cen, axis=-1, keepdims=True)
    y = cen * lax.rsqrt(var + EPS) * lng_ref[...] + lnb_ref[...]
    y = y * jax.nn.sigmoid(y)
    y_conv = jnp.dot(y.astype(BF16), pww_ref[...], preferred_element_type=F32) + pwb_ref[...]

    lane = lax.broadcasted_iota(jnp.int32, (tm, D_POOL), 1)
    win = jnp.left_shift(2, lax.shift_right_logical(lane, int(np.log2(POOL_GROUP))))
    pos = s * tm + lax.broadcasted_iota(jnp.int32, (tm, D_POOL), 0)
    count = jnp.minimum(pos + 1, win).astype(F32)
    cur = pbuf[HALO:HALO + tm, :]
    assert POOL_WINDOWS == tuple(2 << l for l in range(N_POOL_GROUPS)) and HALO >= 8 * N_POOL_GROUPS
    src, wsum = pbuf, None
    for level in range(N_POOL_GROUPS):
        w = 1 << level
        lo = SUBLANES * (level + 1)
        if level < N_POOL_GROUPS - 1:
            psum[level, lo:n_rows, :] = src[lo:n_rows, :] + src[lo - w:n_rows - w, :]
            src = psum.at[level]
            s_w = src[HALO:n_rows, :]
        else:
            s_w = src[HALO:n_rows, :] + src[HALO - w:n_rows - w, :]
        wsum = s_w if wsum is None else jnp.where(win == 2 * w, s_w, wsum)
    pooled = wsum / count - cur
    y_pool = jnp.dot(pooled.astype(BF16), poolw_ref[...],
                     preferred_element_type=F32) * pools_ref[...]

    mix = jnp.dot(y_conv.astype(BF16), wout_ref[0:D_CONV, :], preferred_element_type=F32)
    mix += jnp.dot(y_pool.astype(BF16), wout_ref[D_CONV:D_CONV + D_POOL, :],
                   preferred_element_type=F32)
    mix += jnp.dot(ya_ref[...], wout_ref[D_CONV + D_POOL:, :], preferred_element_type=F32)
    o_ref[...] = x_ref[...] + mix

    ubuf[0:HALO, :] = ubuf[tm:tm + HALO, :]
    pbuf[0:HALO, :] = pbuf[tm:tm + HALO, :]


def _mixers(u, p, ya, x2, dww, dwb, lng, lnb, pww, pwb, poolw, pools, wout, layer, batch, seq,
            tm):
    ns = seq // tm
    row = lambda b, s: (b * ns + s, 0)
    const = lambda b, s: (0, 0)
    full = lambda a: pl.BlockSpec(a.shape, const)
    return pl.pallas_call(
        functools.partial(_mix_kernel, tm=tm),
        grid=(batch, ns),
        in_specs=[
            pl.BlockSpec((tm, D_CONV), row),
            pl.BlockSpec((tm, D_POOL), row),
            pl.BlockSpec((tm, D_ATTN), row),
            pl.BlockSpec((tm, D_MODEL), row),
            full(dww), full(dwb), full(lng), full(lnb), _layer(pww, layer), full(pwb),
            _layer(poolw, layer), full(pools), _layer(wout, layer),
        ],
        out_specs=pl.BlockSpec((tm, D_MODEL), row),
        out_shape=jax.ShapeDtypeStruct(x2.shape, F32),
        scratch_shapes=[
            pltpu.VMEM((HALO + tm, D_CONV), F32),
            pltpu.VMEM((HALO + tm, D_POOL), F32),
            pltpu.VMEM((SUBLANES - 1, HALO + tm, D_CONV), F32),
            pltpu.VMEM((N_POOL_GROUPS - 1, HALO + tm, D_POOL), F32),
        ],
        compiler_params=pltpu.CompilerParams(
            dimension_semantics=("arbitrary", "arbitrary"), vmem_limit_bytes=VMEM_LIMIT),
        name="mixers_outproj",
    )(u, p, ya, x2, dww, dwb, lng, lnb, pww, pwb, poolw, pools, wout)


def _shift_rows(g, prev, r):
    rolled = pltpu.roll(g, r, 0)
    prev_r = pltpu.roll(prev, r, 0)
    first = lax.broadcasted_iota(jnp.int32, prev.shape, 0) < r
    top = jnp.where(first, prev_r, rolled[0:SUBLANES, :])
    return jnp.concatenate([top, rolled[SUBLANES:, :]], axis=0)


def _ffn_kernel(x_ref, g2_ref, wv_ref, wg_ref, dww_ref, dwb_ref, wd_ref, fg_ref, o_ref,
                h_ref, prod_ref, carry_ref, *, tm, tf, final_norm):
    s = pl.program_id(1)
    x = x_ref[...]
    h_ref[...] = _rmsnorm(x, g2_ref[...]).astype(BF16)

    @pl.when(s == 0)
    def _():
        carry_ref[...] = jnp.zeros_like(carry_ref)

    for c in range(D_FF // tf):
        cols = slice(c * tf, (c + 1) * tf)
        h = h_ref[...]
        val = jnp.dot(h, wv_ref[:, cols], preferred_element_type=F32)
        g = jnp.dot(h, wg_ref[:, cols], preferred_element_type=F32)
        prev = carry_ref[:, cols]
        carry_ref[:, cols] = g[tm - SUBLANES:, :]
        conv = (dww_ref[0:1, cols] * _shift_rows(g, prev, 2)
                + dww_ref[1:2, cols] * _shift_rows(g, prev, 1)
                + dww_ref[2:3, cols] * g) + dwb_ref[:, cols]
        gate = 0.5 * conv * (1.0 + lax.erf(conv * np.float32(np.sqrt(0.5))))
        prod_ref[:, cols] = (val * gate).astype(BF16)

    y = x + jnp.dot(prod_ref[...], wd_ref[...], preferred_element_type=F32)
    if final_norm:
        y = _rmsnorm(y, fg_ref[...])
    o_ref[...] = y


def _ffn(x2, g2, w_up, dww, dwb, wd, fg, layer, batch, seq, tm, tf, final_norm):
    ns = seq // tm
    row = lambda b, s: (b * ns + s, 0)
    const = lambda b, s: (0, 0)
    full = lambda a: pl.BlockSpec(a.shape, const)
    return pl.pallas_call(
        functools.partial(_ffn_kernel, tm=tm, tf=tf, final_norm=final_norm),
        grid=(batch, ns),
        in_specs=[pl.BlockSpec((tm, D_MODEL), row), full(g2),
                  pl.BlockSpec((None, D_MODEL, D_FF), lambda b, s: (layer, 0, 0)),
                  pl.BlockSpec((None, D_MODEL, D_FF), lambda b, s: (layer, 0, 1)),
                  full(dww), full(dwb), _layer(wd, layer), full(fg)],
        out_specs=pl.BlockSpec((tm, D_MODEL), row),
        out_shape=jax.ShapeDtypeStruct(x2.shape, F32),
        scratch_shapes=[
            pltpu.VMEM((tm, D_MODEL), BF16),
            pltpu.VMEM((tm, D_FF), BF16),
            pltpu.VMEM((SUBLANES, D_FF), F32),
        ],
        compiler_params=pltpu.CompilerParams(
            dimension_semantics=("arbitrary", "arbitrary"), vmem_limit_bytes=VMEM_LIMIT),
        name="conv_ffn",
    )(x2, g2, w_up, w_up, dww, dwb, wd, fg)


def _block_diag(w):
    n, g, c, _ = w.shape
    eye = jnp.eye(g, dtype=w.dtype)
    return (eye[None, :, None, :, None] * w[:, :, :, None, :]).reshape(n, g * c, g * c)


def kernel(x, norm1_g, w_in, conv_dw_w, conv_dw_b, conv_ln_g, conv_ln_b, conv_pw_w, conv_pw_b,
           pool_w, pool_scale, w_out, norm2_g, ffn_up, ffn_dw_w, ffn_dw_b, ffn_down, final_g):
    batch, seq, d = x.shape
    depth = w_in.shape[0]
    x2 = x.reshape(batch * seq, d)
    vec = lambda a: a.reshape(1, -1)
    w_in_b, pw_b, w_out_b = w_in.astype(BF16), conv_pw_w.astype(BF16), w_out.astype(BF16)
    pool_b, up_b, down_b = _block_diag(pool_w).astype(BF16), ffn_up.astype(BF16), ffn_down.astype(BF16)
    for l in range(depth):
        u, p, q, k, v = _inproj(x2, vec(norm1_g[l]), w_in_b, l, tm=1024)
        ya = _attention(q, k, v, batch, seq)
        x2 = _mixers(u, p, ya, x2, conv_dw_w[l], vec(conv_dw_b[l]), vec(conv_ln_g[l]),
                     vec(conv_ln_b[l]), pw_b, vec(conv_pw_b[l]), pool_b, vec(pool_scale[l]),
                     w_out_b, l, batch, seq, tm=1024)
        x2 = _ffn(x2, vec(norm2_g[l]), up_b, ffn_dw_w[l], vec(ffn_dw_b[l]), down_b,
                  vec(final_g), l, batch, seq, tm=512, tf=256, final_norm=(l == depth - 1))
    return x2.reshape(batch, seq, d)
```
